```python
import jax
import jax.numpy as jnp
from jax import lax
import numpy as np

D_MODEL = 2048
BATCH = 4
SEQ = 4096
DEPTH = 1
DEC_BATCH = 32
DEC_SEQ = 8
PAST_LEN = 16384
PAGE_SIZE = 128

CHUNK = 128
A_GROUPS = 8
A_WIDTH = 1024
A_GROUP_DIM = A_WIDTH // A_GROUPS
HEAD_DIM = 128
B_PATTERNS = ((128, 1), (512, 4), (2048, 16))
B_HEADS_PER_GROUP = 4
B_GROUPS = len(B_PATTERNS)
B_HEADS = B_GROUPS * B_HEADS_PER_GROUP
B_QKV = B_HEADS * HEAD_DIM
B_OUT = B_HEADS_PER_GROUP * HEAD_DIM
ATTN_SCALE = HEAD_DIM ** -0.5
N_EXPERTS = 32
TOP_K = 4
D_FF = 2048
SWIGLU_LIMIT = 7.0
SWIGLU_ALPHA = 1.702
MOE_BLOCK = 128
EPS = 1e-6
IN_WIDTH = 2 * A_WIDTH + 3 * B_QKV + 2 * D_MODEL
SPLITS = (A_WIDTH, 2 * A_WIDTH, 2 * A_WIDTH + B_QKV, 2 * A_WIDTH + 2 * B_QKV,
          2 * A_WIDTH + 3 * B_QKV, 2 * A_WIDTH + 3 * B_QKV + D_MODEL)

kernel_name = 'dilated_gmlp_moe_hybrid_step'


def rms_norm(x, g):
    xf = x.astype(jnp.float32)
    y = xf * lax.rsqrt(jnp.mean(xf * xf, axis=-1, keepdims=True) + EPS)
    return (y * g.astype(jnp.float32)).astype(x.dtype)


def layer_norm(x, g, b):
    xf = x.astype(jnp.float32)
    xc = xf - jnp.mean(xf, axis=-1, keepdims=True)
    y = xc * lax.rsqrt(jnp.mean(xc * xc, axis=-1, keepdims=True) + EPS)
    return (y * g.astype(jnp.float32) + b.astype(jnp.float32)).astype(x.dtype)


def token_mixer_inputs(x, norm1, w_in, b_in_gate, ln_g, ln_b, q_gain, k_gain):
    B, S, _ = x.shape
    h = rms_norm(x, norm1)
    ua, va, q, k, v, ga, gb = jnp.split(h @ w_in, SPLITS, axis=-1)
    u = jax.nn.gelu(ua, approximate=False)
    vn = layer_norm(jax.nn.gelu(va, approximate=False), ln_g, ln_b)
    q = rms_norm(q.reshape(B, S, B_HEADS, HEAD_DIM), q_gain)
    k = rms_norm(k.reshape(B, S, B_HEADS, HEAD_DIM), k_gain)
    v = v.reshape(B, S, B_HEADS, HEAD_DIM)
    gate_a = jax.nn.sigmoid(ga + b_in_gate[:D_MODEL])
    gate_b = jax.nn.sigmoid(gb + b_in_gate[D_MODEL:])
    return u, vn, q, k, v, gate_a, gate_b


def gmlp_spatial(u, vn, w_s, b_s, chunk_len):
    B, S, _ = u.shape
    n = S // chunk_len
    ws = (w_s * jnp.tril(jnp.ones((CHUNK, CHUNK), w_s.dtype)))[:, :chunk_len, :chunk_len]
    vg = vn.reshape(B, n, chunk_len, A_GROUPS, A_GROUP_DIM)
    s = jnp.einsum('gts,bnsgc->bntgc', ws, vg) + b_s[:, :chunk_len].T[None, None, :, :, None]
    return u * s.reshape(B, S, A_WIDTH)


def dilated_attn_prompt(q, k, v, window, dil):
    B, S, H, Dh = q.shape
    L = window // dil
    M = S // dil
    nb = -(-M // L)
    Mp = nb * L

    def strided(t):
        return t.reshape(B, M, dil, H, Dh).transpose(0, 2, 1, 3, 4)

    qb = jnp.pad(strided(q), ((0, 0), (0, 0), (0, Mp - M), (0, 0), (0, 0))).reshape(B, dil, nb, L, H, Dh)

    def key_blocks(t):
        tp = jnp.pad(strided(t), ((0, 0), (0, 0), (L, Mp - M), (0, 0), (0, 0))).reshape(B, dil, nb + 1, L, H, Dh)
        return jnp.concatenate([tp[:, :, :-1], tp[:, :, 1:]], axis=3)

    kb = key_blocks(k)
    vb = key_blocks(v)
    qi = jnp.arange(L)[:, None]
    ki = jnp.arange(2 * L)[None, :]
    dist = qi + L - ki
    kpos = jnp.arange(nb)[:, None, None] * L + ki[None] - L
    mask = ((dist >= 0) & (dist <= L))[None] & (kpos >= 0)
    s = jnp.einsum('brnqhd,brnkhd->brnhqk', qb, kb).astype(jnp.float32) * ATTN_SCALE
    s = jnp.where(mask[None, None, :, None], s, -jnp.inf)
    mx = jnp.max(s, axis=-1, keepdims=True)
    p = jnp.exp(s - mx)
    den = jnp.sum(p, axis=-1, keepdims=True)
    o = jnp.einsum('brnhqk,brnkhd->brnqhd', (p / den).astype(v.dtype), vb)
    lse = (mx + jnp.log(den))[..., 0]
    o = o.reshape(B, dil, Mp, H, Dh)[:, :, :M].transpose(0, 2, 1, 3, 4).reshape(B, S, H, Dh)
    lse = lse.transpose(0, 1, 2, 4, 3).reshape(B, dil, Mp, H)[:, :, :M].transpose(0, 2, 1, 3).reshape(B, S, H)
    return o, lse


def dilated_attn_sample(q, k_new, v_new, kv_buf, window, dil):
    Bd, S, H, Dh = q.shape
    Lb = kv_buf.shape[1]
    keys = jnp.concatenate([kv_buf[:, :, 0], k_new], axis=1)
    vals = jnp.concatenate([kv_buf[:, :, 1], v_new], axis=1)
    nk = window // dil + 1
    idx = Lb + jnp.arange(S)[:, None] - dil * jnp.arange(nk)[None, :]
    valid = idx >= 0
    idx = jnp.maximum(idx, 0)
    kg = keys[:, idx]
    vg = vals[:, idx]
    s = jnp.einsum('bqhd,bqkhd->bhqk', q, kg).astype(jnp.float32) * ATTN_SCALE
    s = jnp.where(valid[None, None], s, -jnp.inf)
    mx = jnp.max(s, axis=-1, keepdims=True)
    p = jnp.exp(s - mx)
    den = jnp.sum(p, axis=-1, keepdims=True)
    o = jnp.einsum('bhqk,bqkhd->bqhd', (p / den).astype(v_new.dtype), vg)
    lse = (mx + jnp.log(den))[..., 0].transpose(0, 2, 1)
    return o, lse


def combine_patterns(outs, lses):
    w = jax.nn.softmax(jnp.stack(lses, axis=0), axis=0)
    return jnp.einsum('gbsh,gbshd->bshd', w.astype(outs[0].dtype), jnp.stack(outs, axis=0))


def merge_branches(a_mix, b_heads, gate_a, gate_b, w_a_out, w_b_out, w_o):
    B, S = a_mix.shape[:2]
    a_proj = a_mix @ w_a_out
    b_proj = b_heads.reshape(B, S, B_OUT) @ w_b_out
    return (gate_a * a_proj + gate_b * b_proj) @ w_o


def moe_ffn(h, router_w, router_b, w_gate, b_gate, w_up, b_up, w_down, b_down):
    T, D = h.shape
    logits = h.astype(jnp.float32) @ router_w.astype(jnp.float32) + router_b.astype(jnp.float32)
    top_v, top_e = lax.top_k(logits, TOP_K)
    gates = jax.nn.softmax(top_v, axis=-1).astype(h.dtype)
    n_assign = T * TOP_K
    flat_e = top_e.reshape(n_assign)
    flat_tok = jnp.arange(n_assign, dtype=jnp.int32) // TOP_K
    order = jnp.argsort(flat_e)
    sorted_e = flat_e[order]
    counts = jnp.bincount(flat_e, length=N_EXPERTS)
    starts = jnp.cumsum(counts) - counts
    padded = (counts + MOE_BLOCK - 1) // MOE_BLOCK * MOE_BLOCK
    pad_end = jnp.cumsum(padded)
    pad_start = pad_end - padded
    dest = pad_start[sorted_e] + jnp.arange(n_assign) - starts[sorted_e]
    n_blocks = -(-(n_assign + N_EXPERTS * (MOE_BLOCK - 1)) // MOE_BLOCK)
    n_slots = n_blocks * MOE_BLOCK
    slot_tok = jnp.zeros((n_slots,), jnp.int32).at[dest].set(flat_tok[order])
    slot_gate = jnp.zeros((n_slots,), h.dtype).at[dest].set(gates.reshape(n_assign)[order])
    block_e = jnp.minimum(jnp.searchsorted(pad_end, jnp.arange(n_blocks) * MOE_BLOCK, side='right'), N_EXPERTS - 1)

    def expert_block(args):
        tok, e = args
        xb = h[tok]
        g = jnp.minimum(xb @ w_gate[e] + b_gate[e], SWIGLU_LIMIT)
        u = jnp.clip(xb @ w_up[e] + b_up[e], -SWIGLU_LIMIT, SWIGLU_LIMIT)
        return ((u + 1) * (g * jax.nn.sigmoid(SWIGLU_ALPHA * g))) @ w_down[e] + b_down[e]

    y = lax.map(expert_block, (slot_tok.reshape(n_blocks, MOE_BLOCK), block_e))
    y = y.reshape(n_slots, D) * slot_gate[:, None]
    return jnp.zeros((T, D), h.dtype).at[slot_tok].add(y)


def channel_mixer(x, norm2, router_w, router_b, w_gate, b_gate, w_up, b_up, w_down, b_down):
    B, S, D = x.shape
    h = rms_norm(x, norm2).reshape(B * S, D)
    return moe_ffn(h, router_w, router_b, w_gate, b_gate, w_up, b_up, w_down, b_down).reshape(B, S, D)


def setup_inputs(seed: int = 0) -> dict:
    key = jax.random.key(seed)
    ks = jax.random.split(key, 26)

    def nrm(k, shape, scale):
        return jax.random.normal(k, shape, jnp.float32) * scale

    def kv_shape(w):
        return (DEPTH, DEC_BATCH, min(w, PAST_LEN), 2, B_HEADS_PER_GROUP, HEAD_DIM)

    return {
        'x_prompt': nrm(ks[0], (BATCH, SEQ, D_MODEL), 1.0),
        'x_sample': nrm(ks[1], (DEC_BATCH, DEC_SEQ, D_MODEL), 1.0),
        'cache_kv_w128': nrm(ks[2], kv_shape(B_PATTERNS[0][0]), 1.0),
        'cache_kv_w512': nrm(ks[3], kv_shape(B_PATTERNS[1][0]), 1.0),
        'cache_kv_w2048': nrm(ks[4], kv_shape(B_PATTERNS[2][0]), 1.0),
        'norm1': 1.0 + nrm(ks[5], (DEPTH, D_MODEL), 0.05),
        'w_in': nrm(ks[6], (DEPTH, D_MODEL, IN_WIDTH), D_MODEL ** -0.5),
        'b_in_gate': nrm(ks[7], (DEPTH, 2 * D_MODEL), 0.02),
        'gmlp_ln_g': 1.0 + nrm(ks[8], (DEPTH, A_WIDTH), 0.05),
        'gmlp_ln_b': nrm(ks[9], (DEPTH, A_WIDTH), 0.02),
        'gmlp_w_s': nrm(ks[10], (DEPTH, A_GROUPS, CHUNK, CHUNK), CHUNK ** -0.5),
        'gmlp_b_s': 1.0 + nrm(ks[11], (DEPTH, A_GROUPS, CHUNK), 0.1),
        'q_gain': 1.0 + nrm(ks[12], (DEPTH, HEAD_DIM), 0.05),
        'k_gain': 1.0 + nrm(ks[13], (DEPTH, HEAD_DIM), 0.05),
        'w_a_out': nrm(ks[14], (DEPTH, A_WIDTH, D_MODEL), A_WIDTH ** -0.5),
        'w_b_out': nrm(ks[15], (DEPTH, B_OUT, D_MODEL), B_OUT ** -0.5),
        'w_o': nrm(ks[16], (DEPTH, D_MODEL, D_MODEL), D_MODEL ** -0.5),
        'norm2': 1.0 + nrm(ks[17], (DEPTH, D_MODEL), 0.05),
        'router_w': nrm(ks[18], (DEPTH, D_MODEL, N_EXPERTS), D_MODEL ** -0.5),
        'router_b': nrm(ks[19], (DEPTH, N_EXPERTS), 0.01),
        'exp_w_gate': nrm(ks[20], (DEPTH, N_EXPERTS, D_MODEL, D_FF), D_MODEL ** -0.5),
        'exp_b_gate': nrm(ks[21], (DEPTH, N_EXPERTS, D_FF), 0.02),
        'exp_w_up': nrm(ks[22], (DEPTH, N_EXPERTS, D_MODEL, D_FF), D_MODEL ** -0.5),
        'exp_b_up': nrm(ks[23], (DEPTH, N_EXPERTS, D_FF), 0.02),
        'exp_w_down': nrm(ks[24], (DEPTH, N_EXPERTS, D_FF, D_MODEL), D_FF ** -0.5),
        'exp_b_down': nrm(ks[25], (DEPTH, N_EXPERTS, D_MODEL), 0.02),
    }


def reference(x_prompt, x_sample, cache_kv_w128, cache_kv_w512, cache_kv_w2048,
              norm1, w_in, b_in_gate, gmlp_ln_g, gmlp_ln_b, gmlp_w_s, gmlp_b_s,
              q_gain, k_gain, w_a_out, w_b_out, w_o, norm2, router_w, router_b,
              exp_w_gate, exp_b_gate, exp_w_up, exp_b_up, exp_w_down, exp_b_down):
    caches = (cache_kv_w128, cache_kv_w512, cache_kv_w2048)
    seq_p = x_prompt.shape[1]
    seq_s = x_sample.shape[1]
    xp = x_prompt
    xs = x_sample
    kv_p = [[] for _ in B_PATTERNS]
    kv_s = [[] for _ in B_PATTERNS]
    v_s = []
    for l in range(DEPTH):
        mix_in = (norm1[l], w_in[l], b_in_gate[l], gmlp_ln_g[l], gmlp_ln_b[l], q_gain[l], k_gain[l])
        mix_out = (w_a_out[l], w_b_out[l], w_o[l])
        ffn = (norm2[l], router_w[l], router_b[l], exp_w_gate[l], exp_b_gate[l],
               exp_w_up[l], exp_b_up[l], exp_w_down[l], exp_b_down[l])

        u, vn, q, k, v, ga, gb = token_mixer_inputs(xp, *mix_in)
        a_mix = gmlp_spatial(u, vn, gmlp_w_s[l], gmlp_b_s[l], CHUNK)
        outs, lses = [], []
        for g, (win, dil) in enumerate(B_PATTERNS):
            hs = slice(g * B_HEADS_PER_GROUP, (g + 1) * B_HEADS_PER_GROUP)
            o, lse = dilated_attn_prompt(q[:, :, hs], k[:, :, hs], v[:, :, hs], win, dil)
            outs.append(o)
            lses.append(lse)
            keep = min(win, seq_p)
            kv_p[g].append(jnp.stack([k[:, seq_p - keep:, hs], v[:, seq_p - keep:, hs]], axis=2))
        xp = xp + merge_branches(a_mix, combine_patterns(outs, lses), ga, gb, *mix_out)
        xp = xp + channel_mixer(xp, *ffn)

        u, vn, q, k, v, ga, gb = token_mixer_inputs(xs, *mix_in)
        a_mix = gmlp_spatial(u, vn, gmlp_w_s[l], gmlp_b_s[l], seq_s)
        v_s.append(vn)
        outs, lses = [], []
        for g, (win, dil) in enumerate(B_PATTERNS):
            hs = slice(g * B_HEADS_PER_GROUP, (g + 1) * B_HEADS_PER_GROUP)
            o, lse = dilated_attn_sample(q[:, :, hs], k[:, :, hs], v[:, :, hs], caches[g][l], win, dil)
            outs.append(o)
            lses.append(lse)
            kv_s[g].append(jnp.stack([k[:, :, hs], v[:, :, hs]], axis=2))
        xs = xs + merge_branches(a_mix, combine_patterns(outs, lses), ga, gb, *mix_out)
        xs = xs + channel_mixer(xs, *ffn)

    return (xp, xs,
            jnp.stack(kv_p[0]), jnp.stack(kv_p[1]), jnp.stack(kv_p[2]),
            jnp.stack(kv_s[0]), jnp.stack(kv_s[1]), jnp.stack(kv_s[2]),
            jnp.stack(v_s))
```

```python
import functools

import jax
import jax.numpy as jnp
from jax import lax
from jax.experimental import pallas as pl
from jax.experimental.pallas import tpu as pltpu

F32 = jnp.float32
BF16 = jnp.bfloat16
U32 = jnp.uint32
I32 = jnp.int32

D_MODEL = 2048
CHUNK = 128
A_GROUPS = 8
A_WIDTH = 1024
HEAD_DIM = 128
B_PATTERNS = ((128, 1), (512, 4), (2048, 16))
HEADS_PER_GROUP = 4
B_HEADS = len(B_PATTERNS) * HEADS_PER_GROUP
B_QKV = B_HEADS * HEAD_DIM
B_OUT = HEADS_PER_GROUP * HEAD_DIM
ATTN_SCALE = HEAD_DIM ** -0.5
N_EXPERTS = 32
TOP_K = 4
D_FF = 2048
SWIGLU_LIMIT = 7.0
SWIGLU_ALPHA = 1.702
EPS = 1e-6
SQRT_HALF = 0.7071067811865476

COL_UV = 0
COL_Q = 2 * A_WIDTH
COL_K = COL_Q + B_QKV
COL_V = COL_K + B_QKV
COL_GATE = COL_V + B_QKV

LANES = 128
MOE_SUPER = 1024
MOE_SUB = 256
MOE_TF = 512
MIB = 1 << 20


def _params(semantics, vmem_mib):
    return pltpu.CompilerParams(dimension_semantics=semantics, vmem_limit_bytes=vmem_mib * MIB)


def _pack_bf16_pair(a, b):
    au = lax.bitcast_convert_type(a.astype(BF16).astype(F32), U32)
    bu = lax.bitcast_convert_type(b.astype(BF16).astype(F32), U32)
    return au | (bu >> 16)


def _unpack_bf16_pair(p):
    a = lax.bitcast_convert_type(p & jnp.uint32(0xFFFF0000), F32)
    b = lax.bitcast_convert_type(p << 16, F32)
    return a, b


def _rmsnorm_kernel(x_ref, g_ref, o_ref):
    x = x_ref[...]
    y = x * lax.rsqrt(jnp.mean(x * x, axis=-1, keepdims=True) + EPS)
    o_ref[...] = (y * g_ref[...]).astype(o_ref.dtype)


def _rmsnorm_bf16(x, g, tm):
    t, d = x.shape
    return pl.pallas_call(
        _rmsnorm_kernel,
        grid=(t // tm,),
        in_specs=[pl.BlockSpec((tm, d), lambda i: (i, 0)), pl.BlockSpec((1, d), lambda i: (0, 0))],
        out_specs=pl.BlockSpec((tm, d), lambda i: (i, 0)),
        out_shape=jax.ShapeDtypeStruct((t, d), BF16),
        compiler_params=_params(("parallel",), 32),
        name="rmsnorm1",
    )(x, g.reshape(1, d))


def _cast_kernel(w_ref, o_ref):
    o_ref[...] = w_ref[...].astype(o_ref.dtype)


def _to_bf16(w, tr=256):
    r, c = w.shape
    return pl.pallas_call(
        _cast_kernel,
        grid=(r // tr,),
        in_specs=[pl.BlockSpec((tr, c), lambda i: (i, 0))],
        out_specs=pl.BlockSpec((tr, c), lambda i: (i, 0)),
        out_shape=jax.ShapeDtypeStruct((r, c), BF16),
        compiler_params=_params(("parallel",), 32),
        name="cast_bf16",
    )(w)


def _ep_gelu(acc):
    return 0.5 * acc * (1.0 + lax.erf(acc * SQRT_HALF))


def _ep_identity(acc):
    return acc


def _ep_headnorm(acc, gain_ref):
    outs = []
    for h in range(acc.shape[1] // HEAD_DIM):
        a = acc[:, h * HEAD_DIM:(h + 1) * HEAD_DIM]
        ms = jnp.mean(a * a, axis=-1, keepdims=True)
        outs.append(a * lax.rsqrt(ms + EPS) * gain_ref[...])
    return jnp.concatenate(outs, axis=1)


def _ep_gate(acc, bias_ref):
    return jax.nn.sigmoid(acc + bias_ref[...])


def _proj_kernel(h_ref, w_ref, *rest, epilogue, n_extra):
    extra = rest[:n_extra]
    o_ref = rest[n_extra]
    wb_ref = rest[n_extra + 1]

    @pl.when(pl.program_id(1) == 0)
    def _():
        wb_ref[...] = w_ref[...].astype(BF16)

    acc = jnp.dot(h_ref[...], wb_ref[...], preferred_element_type=F32)
    o_ref[...] = epilogue(acc, *extra).astype(o_ref.dtype)


def _in_proj(h, w, col0, ncols, epilogue, extras, extra_specs, out_dtype, tm, name, tn=512):
    t, k = h.shape
    j0 = col0 // tn
    return pl.pallas_call(
        functools.partial(_proj_kernel, epilogue=epilogue, n_extra=len(extras)),
        grid=(ncols // tn, t // tm),
        in_specs=[pl.BlockSpec((tm, k), lambda j, i: (i, 0)),
                  pl.BlockSpec((k, tn), lambda j, i: (0, j0 + j))] + extra_specs,
        out_specs=pl.BlockSpec((tm, tn), lambda j, i: (i, j)),
        out_shape=jax.ShapeDtypeStruct((t, ncols), out_dtype),
        scratch_shapes=[pltpu.VMEM((k, tn), BF16)],
        compiler_params=_params(("arbitrary", "arbitrary"), 48),
        name=name,
    )(h, w, *extras)


def _token_mixer_inputs(x, norm1, w_in, b_in_gate, q_gain, k_gain, tm, tag):
    h = _rmsnorm_bf16(x, norm1, min(tm, 512))
    gain_spec = [pl.BlockSpec((1, HEAD_DIM), lambda j, i: (0, 0))]
    uv = _in_proj(h, w_in, COL_UV, 2 * A_WIDTH, _ep_gelu, [], [], BF16, tm, "proj_uv_" + tag)
    q = _in_proj(h, w_in, COL_Q, B_QKV, _ep_headnorm, [q_gain.reshape(1, HEAD_DIM)], gain_spec, BF16, tm,
                 "proj_q_" + tag)
    k = _in_proj(h, w_in, COL_K, B_QKV, _ep_headnorm, [k_gain.reshape(1, HEAD_DIM)], gain_spec, F32, tm,
                 "proj_k_" + tag)
    v = _in_proj(h, w_in, COL_V, B_QKV, _ep_identity, [], [], F32, tm, "proj_v_" + tag)
    gates = _in_proj(h, w_in, COL_GATE, 2 * D_MODEL, _ep_gate, [b_in_gate.reshape(1, 2 * D_MODEL)],
                     [pl.BlockSpec((1, 512), lambda j, i: (0, j))], BF16, tm, "proj_gate_" + tag)
    return uv, q, k, v, gates


def _gmlp_kernel(uv_ref, lng_ref, lnb_ref, w_ref, bias_ref, a_ref, *vn_out, sample, rows):
    v = uv_ref[:, A_WIDTH:].astype(F32)
    xc = v - jnp.mean(v, axis=-1, keepdims=True)
    vn = xc * lax.rsqrt(jnp.mean(xc * xc, axis=-1, keepdims=True) + EPS) * lng_ref[...] + lnb_ref[...]
    if vn_out:
        vn_out[0][...] = vn
    vnb = vn.astype(BF16)
    row = lax.broadcasted_iota(I32, (CHUNK, CHUNK), 0)
    col = lax.broadcasted_iota(I32, (CHUNK, CHUNK), 1)
    if sample:
        mask = ((row >> 3) == (col >> 3)) & ((row & 7) >= (col & 7))
    else:
        mask = row >= col
    for g in range(A_GROUPS):
        wm = jnp.where(mask, w_ref[g], 0.0).astype(BF16)
        bcol = bias_ref[:, g:g + 1]
        cs = slice(g * CHUNK, (g + 1) * CHUNK)
        for c in range(rows // CHUNK):
            rs = slice(c * CHUNK, (c + 1) * CHUNK)
            s = jnp.dot(wm, vnb[rs, cs], preferred_element_type=F32) + bcol
            a_ref[rs, cs] = (uv_ref[rs, cs].astype(F32) * s).astype(a_ref.dtype)


def _gmlp(uv, ln_g, ln_b, w_mix, bias_t, sample, tag, rows=256):
    t = uv.shape[0]
    out_shape = [jax.ShapeDtypeStruct((t, A_WIDTH), BF16)]
    out_specs = [pl.BlockSpec((rows, A_WIDTH), lambda i: (i, 0))]
    if sample:
        out_shape.append(jax.ShapeDtypeStruct((t, A_WIDTH), F32))
        out_specs.append(pl.BlockSpec((rows, A_WIDTH), lambda i: (i, 0)))
    res = pl.pallas_call(
        functools.partial(_gmlp_kernel, sample=sample, rows=rows),
        grid=(t // rows,),
        in_specs=[pl.BlockSpec((rows, 2 * A_WIDTH), lambda i: (i, 0)),
                  pl.BlockSpec((1, A_WIDTH), lambda i: (0, 0)),
                  pl.BlockSpec((1, A_WIDTH), lambda i: (0, 0)),
                  pl.BlockSpec((A_GROUPS, CHUNK, CHUNK), lambda i: (0, 0, 0)),
                  pl.BlockSpec((CHUNK, A_GROUPS), lambda i: (0, 0))],
        out_specs=out_specs,
        out_shape=out_shape,
        compiler_params=_params(("parallel",), 32),
        name="gmlp_" + tag,
    )(uv, ln_g.reshape(1, A_WIDTH), ln_b.reshape(1, A_WIDTH), w_mix, bias_t)
    return res if sample else res[0]


def _attn_prompt_kernel(q_ref, k_ref, v_ref, o_ref, lse_ref, kbuf, vbuf):
    n = pl.program_id(2)
    L = CHUNK

    @pl.when(n == 0)
    def _():
        kbuf[0:L, :] = jnp.zeros((L, B_OUT), BF16)
        vbuf[0:L, :] = jnp.zeros((L, B_OUT), BF16)

    @pl.when(n > 0)
    def _():
        kbuf[0:L, :] = kbuf[L:2 * L, :]
        vbuf[0:L, :] = vbuf[L:2 * L, :]

    kbuf[L:2 * L, :] = k_ref[0].astype(BF16)
    vbuf[L:2 * L, :] = v_ref[0].astype(BF16)

    qi = lax.broadcasted_iota(I32, (L, 2 * L), 0)
    ki = lax.broadcasted_iota(I32, (L, 2 * L), 1)
    dist = qi + L - ki
    first_key = jnp.where(n > 0, 0, L)
    valid = (dist >= 0) & (dist <= L) & (ki >= first_key)
    lane = lax.broadcasted_iota(I32, (L, LANES), 1)
    lse_tile = jnp.zeros((L, LANES), F32)
    for h in range(HEADS_PER_GROUP):
        cs = slice(h * HEAD_DIM, (h + 1) * HEAD_DIM)
        s = lax.dot_general(q_ref[0, :, cs], kbuf[:, cs], (((1,), (1,)), ((), ())),
                            preferred_element_type=F32) * ATTN_SCALE
        s = jnp.where(valid, s, -jnp.inf)
        m = jnp.max(s, axis=-1, keepdims=True)
        p = jnp.exp(s - m)
        den = jnp.sum(p, axis=-1, keepdims=True)
        o = jnp.dot(p.astype(BF16), vbuf[:, cs], preferred_element_type=F32) / den
        o_ref[0, :, cs] = o.astype(o_ref.dtype)
        lse_tile = jnp.where(lane == h, m + jnp.log(den), lse_tile)
    lse_ref[0] = lse_tile


def _attn_prompt(q, k, v, g, batch, seq):
    _, dil = B_PATTERNS[g]
    m = seq // dil
    nb = m // CHUNK
    nqb = B_QKV // B_OUT
    qv = q.reshape(batch, m, dil * B_QKV)
    kv = k.reshape(batch, m, dil * B_QKV)
    vv = v.reshape(batch, m, dil * B_QKV)
    in_map = lambda b, r, n: (b, n, r * nqb + g)
    out_map = lambda b, r, n: (b, n, r)
    o, lse = pl.pallas_call(
        _attn_prompt_kernel,
        grid=(batch, dil, nb),
        in_specs=[pl.BlockSpec((1, CHUNK, B_OUT), in_map)] * 3,
        out_specs=[pl.BlockSpec((1, CHUNK, B_OUT), out_map), pl.BlockSpec((1, CHUNK, LANES), out_map)],
        out_shape=[jax.ShapeDtypeStruct((batch, m, dil * B_OUT), BF16),
                   jax.ShapeDtypeStruct((batch, m, dil * LANES), F32)],
        scratch_shapes=[pltpu.VMEM((2 * CHUNK, B_OUT), BF16), pltpu.VMEM((2 * CHUNK, B_OUT), BF16)],
        compiler_params=_params(("parallel", "parallel", "arbitrary"), 32),
        name="attn_prompt_g%d" % g,
    )(qv, kv, vv)
    return o.reshape(batch * seq, B_OUT), lse.reshape(batch * seq, LANES)


def _attn_sample_kernel(q_ref, k_ref, v_ref, c1_ref, c2_ref, c3_ref, b_ref):
    S = q_ref.shape[1]
    nres = c3_ref.shape[2] // (2 * B_OUT)
    kv_w = 2 * B_OUT
    outs, lses = [], []
    for g, (win, dil) in enumerate(B_PATTERNS):
        o_heads, l_heads = [], []
        for h in range(HEADS_PER_GROUP):
            qs = slice((g * HEADS_PER_GROUP + h) * HEAD_DIM, (g * HEADS_PER_GROUP + h + 1) * HEAD_DIM)
            hs = slice(h * HEAD_DIM, (h + 1) * HEAD_DIM)
            vs = slice(B_OUT + h * HEAD_DIM, B_OUT + (h + 1) * HEAD_DIM)
            pad = jnp.zeros((CHUNK - S, HEAD_DIM), F32)
            if g < 2:
                cref = c1_ref if g == 0 else c2_ref
                kparts = [cref[0, :, hs]]
                vparts = [cref[0, :, vs]]
                ncache = win
            else:
                kparts = [c3_ref[0, :, r * kv_w + h * HEAD_DIM: r * kv_w + (h + 1) * HEAD_DIM] for r in range(nres)]
                vparts = [c3_ref[0, :, r * kv_w + B_OUT + h * HEAD_DIM: r * kv_w + B_OUT + (h + 1) * HEAD_DIM]
                          for r in range(nres)]
                ncache = nres * CHUNK
            kcat = jnp.concatenate(kparts + [k_ref[0, :, qs], pad], axis=0).astype(BF16)
            vcat = jnp.concatenate(vparts + [v_ref[0, :, qs], pad], axis=0).astype(BF16)
            nk = ncache + CHUNK
            col = lax.broadcasted_iota(I32, (S, nk), 1)
            sq = lax.broadcasted_iota(I32, (S, nk), 0)
            if g < 2:
                valid = (col >= sq) & (col <= win + sq) & (((col - sq) & (dil - 1)) == 0)
            else:
                valid = ((col < ncache) & ((col >> 7) == sq)) | ((col - ncache) == sq)
            s = lax.dot_general(q_ref[0, :, qs], kcat, (((1,), (1,)), ((), ())),
                                preferred_element_type=F32) * ATTN_SCALE
            s = jnp.where(valid, s, -jnp.inf)
            m = jnp.max(s, axis=-1, keepdims=True)
            p = jnp.exp(s - m)
            den = jnp.sum(p, axis=-1, keepdims=True)
            o_heads.append(jnp.dot(p.astype(BF16), vcat, preferred_element_type=F32) / den)
            l_heads.append(m + jnp.log(den))
        outs.append(o_heads)
        lses.append(l_heads)
    comb = []
    for h in range(HEADS_PER_GROUP):
        mx = jnp.maximum(jnp.maximum(lses[0][h], lses[1][h]), lses[2][h])
        e = [jnp.exp(lses[g][h] - mx) for g in range(3)]
        den = e[0] + e[1] + e[2]
        comb.append((e[0] / den) * outs[0][h] + (e[1] / den) * outs[1][h] + (e[2] / den) * outs[2][h])
    b_ref[0] = jnp.concatenate(comb, axis=1).astype(b_ref.dtype)


def _attn_sample(q, k, v, c1, c2, c3, batch, seq):
    kv_w = 2 * B_OUT
    win3, dil3 = B_PATTERNS[2]
    assert seq <= dil3 and seq <= CHUNK, "decode queries must hit distinct residues of the widest pattern"
    assert c1.shape[1] == B_PATTERNS[0][0] and c2.shape[1] == B_PATTERNS[1][0] and c3.shape[1] == win3
    nres = 8
    assert seq <= nres
    c3v = c3.reshape(batch, win3 // dil3, dil3 * kv_w)
    bmap = lambda b: (b, 0, 0)
    return pl.pallas_call(
        _attn_sample_kernel,
        grid=(batch,),
        in_specs=[pl.BlockSpec((1, seq, B_QKV), bmap)] * 3 + [
            pl.BlockSpec((1, c1.shape[1], kv_w), bmap),
            pl.BlockSpec((1, c2.shape[1], kv_w), bmap),
            pl.BlockSpec((1, win3 // dil3, nres * kv_w), bmap)],
        out_specs=pl.BlockSpec((1, seq, B_OUT), bmap),
        out_shape=jax.ShapeDtypeStruct((batch, seq, B_OUT), BF16),
        compiler_params=_params(("parallel",), 40),
        name="attn_sample",
    )(q.reshape(batch, seq, B_QKV), k.reshape(batch, seq, B_QKV), v.reshape(batch, seq, B_QKV), c1, c2, c3v)


def _combine_groups(o_refs, l_refs):
    comb = []
    for h in range(HEADS_PER_GROUP):
        cs = slice(h * HEAD_DIM, (h + 1) * HEAD_DIM)
        ls = [l[:, h:h + 1] for l in l_refs]
        mx = jnp.maximum(jnp.maximum(ls[0], ls[1]), ls[2])
        e = [jnp.exp(l - mx) for l in ls]
        den = e[0] + e[1] + e[2]
        comb.append(sum((e[g] / den) * o_refs[g][:, cs].astype(F32) for g in range(3)))
    return jnp.concatenate(comb, axis=1).astype(BF16)


def _mix_kernel(*refs, combine):
    if combine:
        a_ref = refs[0]
        b = _combine_groups(refs[1:4], refs[4:7])
        ga_ref, gb_ref, wa_ref, wb_ref, m_ref = refs[7:]
    else:
        a_ref, b_ref, ga_ref, gb_ref, wa_ref, wb_ref, m_ref = refs
        b = b_ref[...]
    ap = jnp.dot(a_ref[...], wa_ref[...], preferred_element_type=F32)
    bp = jnp.dot(b, wb_ref[...], preferred_element_type=F32)
    m_ref[...] = (ga_ref[...].astype(F32) * ap + gb_ref[...].astype(F32) * bp).astype(m_ref.dtype)


def _mix(a, b_parts, gates, wa, wb, tm, tag):
    t = a.shape[0]
    combine = len(b_parts) > 1
    row = lambda w: pl.BlockSpec((tm, w), lambda i: (i, 0))
    if combine:
        b_specs = [row(B_OUT)] * 3 + [row(LANES)] * 3
    else:
        b_specs = [row(B_OUT)]
    return pl.pallas_call(
        functools.partial(_mix_kernel, combine=combine),
        grid=(t // tm,),
        in_specs=[row(A_WIDTH)] + b_specs + [
            pl.BlockSpec((tm, D_MODEL), lambda i: (i, 0)),
            pl.BlockSpec((tm, D_MODEL), lambda i: (i, 1)),
            pl.BlockSpec((A_WIDTH, D_MODEL), lambda i: (0, 0)),
            pl.BlockSpec((B_OUT, D_MODEL), lambda i: (0, 0))],
        out_specs=row(D_MODEL),
        out_shape=jax.ShapeDtypeStruct((t, D_MODEL), BF16),
        compiler_params=_params(("parallel",), 48),
        name="mix_" + tag,
    )(a, *b_parts, gates, gates, wa, wb)


def _out_kernel(x_ref, m_ref, wo_ref, g2_ref, wr_ref, rb_ref, h2_in, lg_in, x1_ref, h2_ref, lg_ref):
    del h2_in, lg_in
    x1 = x_ref[...] + jnp.dot(m_ref[...], wo_ref[...], preferred_element_type=F32)
    x1_ref[...] = x1
    h = x1 * lax.rsqrt(jnp.mean(x1 * x1, axis=-1, keepdims=True) + EPS) * g2_ref[...]
    lg_ref[...] = jnp.dot(h, wr_ref[...], preferred_element_type=F32,
                          precision=lax.Precision.HIGHEST) + rb_ref[...]
    h2_ref[...] = _pack_bf16_pair(h[:, :D_MODEL // 2], h[:, D_MODEL // 2:])


def _out_proj(x, m, wo, norm2, router_w, router_b, h2_buf, lg_buf, row0, tm, tag):
    t = x.shape[0]
    blk0 = row0 // tm
    const = lambda r, c: pl.BlockSpec((r, c), lambda i: (0, 0))
    return pl.pallas_call(
        _out_kernel,
        grid=(t // tm,),
        in_specs=[pl.BlockSpec((tm, D_MODEL), lambda i: (i, 0)),
                  pl.BlockSpec((tm, D_MODEL), lambda i: (i, 0)),
                  const(D_MODEL, D_MODEL), const(1, D_MODEL), const(D_MODEL, N_EXPERTS), const(1, N_EXPERTS),
                  pl.BlockSpec(memory_space=pl.ANY), pl.BlockSpec(memory_space=pl.ANY)],
        out_specs=[pl.BlockSpec((tm, D_MODEL), lambda i: (i, 0)),
                   pl.BlockSpec((tm, D_MODEL // 2), lambda i: (blk0 + i, 0)),
                   pl.BlockSpec((tm, N_EXPERTS), lambda i: (blk0 + i, 0))],
        out_shape=[jax.ShapeDtypeStruct((t, D_MODEL), F32),
                   jax.ShapeDtypeStruct(h2_buf.shape, U32),
                   jax.ShapeDtypeStruct(lg_buf.shape, F32)],
        input_output_aliases={6: 1, 7: 2},
        compiler_params=_params(("parallel",), 48),
        name="out_proj_" + tag,
    )(x, m, wo, norm2.reshape(1, D_MODEL), router_w, router_b.reshape(1, N_EXPERTS), h2_buf, lg_buf)


def _route_kernel(lg_ref, idx_ref, gate_ref, cnt_ref, *, rows):
    @pl.when(pl.program_id(0) == 0)
    def _():
        cnt_ref[...] = jnp.zeros_like(cnt_ref)

    l = lg_ref[...]
    lane = lax.broadcasted_iota(I32, l.shape, 1)
    vals, idxs = [], []
    for _ in range(TOP_K):
        m = jnp.max(l, axis=-1, keepdims=True)
        idx = jnp.min(jnp.where(l == m, lane, N_EXPERTS), axis=-1, keepdims=True)
        vals.append(m)
        idxs.append(idx)
        l = jnp.where(lane == idx, -jnp.inf, l)
    es = [jnp.exp(v - vals[0]) for v in vals]
    den = es[0] + es[1] + es[2] + es[3]
    onehot = sum((lane == idx).astype(F32) for idx in idxs)
    r = lax.broadcasted_iota(I32, (rows, rows), 0)
    c = lax.broadcasted_iota(I32, (rows, rows), 1)
    tri = (r > c).astype(BF16)
    before = jnp.dot(tri, onehot.astype(BF16), preferred_element_type=F32) + cnt_ref[...]
    cnt_ref[...] += jnp.sum(onehot, axis=0, keepdims=True)
    wide = lax.broadcasted_iota(I32, (rows, LANES), 1)
    idx_tile = jnp.zeros((rows, LANES), I32)
    gate_tile = jnp.zeros((rows, LANES), F32)
    for k in range(TOP_K):
        rank = jnp.sum(jnp.where(lane == idxs[k], before, 0.0), axis=-1, keepdims=True).astype(I32)
        idx_tile = jnp.where(wide == k, idxs[k], idx_tile)
        idx_tile = jnp.where(wide == TOP_K + k, rank, idx_tile)
        gate_tile = jnp.where(wide == k, es[k] / den, gate_tile)
    idx_ref[...] = idx_tile
    gate_ref[...] = gate_tile


def _route(logits, rows=256):
    t = logits.shape[0]
    return pl.pallas_call(
        functools.partial(_route_kernel, rows=rows),
        grid=(t // rows,),
        in_specs=[pl.BlockSpec((rows, N_EXPERTS), lambda i: (i, 0))],
        out_specs=[pl.BlockSpec((rows, LANES), lambda i: (i, 0)),
                   pl.BlockSpec((rows, LANES), lambda i: (i, 0)),
                   pl.BlockSpec((1, N_EXPERTS), lambda i: (0, 0))],
        out_shape=[jax.ShapeDtypeStruct((t, LANES), I32),
                   jax.ShapeDtypeStruct((t, LANES), F32),
                   jax.ShapeDtypeStruct((1, N_EXPERTS), F32)],
        compiler_params=_params(("arbitrary",), 32),
        name="route",
    )(logits)


def _dispatch_kernel(dest_ref, h_hbm, xs_hbm, sem, *, tm):
    base = pl.program_id(0) * tm

    def body(t, carry):
        src = h_hbm.at[pl.ds(base + t, 1)]
        for k in range(TOP_K):
            pltpu.make_async_copy(src, xs_hbm.at[pl.ds(dest_ref[0, 0, t * TOP_K + k], 1)], sem).start()
        return carry

    lax.fori_loop(0, tm, body, 0)
    for k in range(TOP_K):
        pltpu.make_async_copy(h_hbm.at[pl.ds(0, tm)], xs_hbm.at[pl.ds(0, tm)], sem).wait()


def _dispatch(h2, dest, n_slots, tm=256):
    t, w = h2.shape
    return pl.pallas_call(
        functools.partial(_dispatch_kernel, tm=tm),
        grid=(t // tm,),
        in_specs=[pl.BlockSpec((1, 1, tm * TOP_K), lambda i: (i, 0, 0), memory_space=pltpu.SMEM),
                  pl.BlockSpec(memory_space=pl.ANY)],
        out_specs=pl.BlockSpec(memory_space=pl.ANY),
        out_shape=jax.ShapeDtypeStruct((n_slots, w), h2.dtype),
        scratch_shapes=[pltpu.SemaphoreType.DMA],
        compiler_params=_params(("arbitrary",), 32),
        name="dispatch",
    )(dest.reshape(t // tm, 1, tm * TOP_K), h2)


def _moe_kernel(be_ref, bx_ref, nv_ref, x_ref, wg_ref, wu_ref, wd_ref, bg_ref, bu_ref, bd_ref, o_ref, acc_ref,
                *, n_f):
    del be_ref, bx_ref
    f = pl.program_id(1)
    nv = nv_ref[pl.program_id(0)]

    @pl.when(f == 0)
    def _():
        acc_ref[...] = jnp.zeros_like(acc_ref)

    for s in range(MOE_SUPER // MOE_SUB):
        @pl.when(s < nv)
        def _():
            rows = pl.ds(s * MOE_SUB, MOE_SUB)
            lo, hi = _unpack_bf16_pair(x_ref[rows, :])
            xb = jnp.concatenate([lo.astype(BF16), hi.astype(BF16)], axis=1)
            g = jnp.dot(xb, wg_ref[0].astype(BF16), preferred_element_type=F32) + bg_ref[0]
            u = jnp.dot(xb, wu_ref[0].astype(BF16), preferred_element_type=F32) + bu_ref[0]
            g = jnp.minimum(g, SWIGLU_LIMIT)
            u = jnp.clip(u, -SWIGLU_LIMIT, SWIGLU_LIMIT)
            act = ((u + 1.0) * (g * jax.nn.sigmoid(SWIGLU_ALPHA * g))).astype(BF16)
            acc_ref[rows, :] += jnp.dot(act, wd_ref[0].astype(BF16), preferred_element_type=F32)

            @pl.when(f == n_f - 1)
            def _():
                y = acc_ref[rows, :] + bd_ref[0]
                o_ref[rows, :] = _pack_bf16_pair(y[:, :D_MODEL // 2], y[:, D_MODEL // 2:])


def _moe(xs, blk_e, blk_x, blk_nv, w_gate, b_gate, w_up, b_up, w_down, b_down):
    n_super = xs.shape[0] // MOE_SUPER
    n_f = D_FF // MOE_TF
    last_f = n_f - 1

    def f_of(m, f, nv):
        return jnp.where(nv[m] > 0, f, last_f)

    grid_spec = pltpu.PrefetchScalarGridSpec(
        num_scalar_prefetch=3,
        grid=(n_super, n_f),
        in_specs=[
            pl.BlockSpec((MOE_SUPER, D_MODEL // 2), lambda m, f, be, bx, nv: (bx[m], 0)),
            pl.BlockSpec((1, D_MODEL, MOE_TF), lambda m, f, be, bx, nv: (be[m], 0, f_of(m, f, nv))),
            pl.BlockSpec((1, D_MODEL, MOE_TF), lambda m, f, be, bx, nv: (be[m], 0, f_of(m, f, nv))),
            pl.BlockSpec((1, MOE_TF, D_MODEL), lambda m, f, be, bx, nv: (be[m], f_of(m, f, nv), 0)),
            pl.BlockSpec((1, 1, MOE_TF), lambda m, f, be, bx, nv: (be[m], 0, f_of(m, f, nv))),
            pl.BlockSpec((1, 1, MOE_TF), lambda m, f, be, bx, nv: (be[m], 0, f_of(m, f, nv))),
            pl.BlockSpec((1, 1, D_MODEL), lambda m, f, be, bx, nv: (be[m], 0, 0)),
        ],
        out_specs=pl.BlockSpec((MOE_SUPER, D_MODEL // 2), lambda m, f, be, bx, nv: (bx[m], 0)),
        scratch_shapes=[pltpu.VMEM((MOE_SUPER, D_MODEL), F32)],
    )
    return pl.pallas_call(
        functools.partial(_moe_kernel, n_f=n_f),
        grid_spec=grid_spec,
        out_shape=jax.ShapeDtypeStruct(xs.shape, U32),
        compiler_params=_params(("arbitrary", "arbitrary"), 56),
        name="moe_ffn",
    )(blk_e, blk_x, blk_nv, xs, w_gate, w_up, w_down,
      b_gate.reshape(N_EXPERTS, 1, D_FF), b_up.reshape(N_EXPERTS, 1, D_FF), b_down.reshape(N_EXPERTS, 1, D_MODEL))


def _combine_kernel(dest_ref, gate_ref, x1_ref, y_hbm, o_ref, buf, sem, *, tm):
    def body(t, carry):
        for k in range(TOP_K):
            pltpu.make_async_copy(y_hbm.at[pl.ds(dest_ref[0, 0, t * TOP_K + k], 1)],
                                  buf.at[k, pl.ds(t, 1)], sem).start()
        return carry

    lax.fori_loop(0, tm, body, 0)
    for k in range(TOP_K):
        pltpu.make_async_copy(y_hbm.at[pl.ds(0, tm)], buf.at[k], sem).wait()
    half = D_MODEL // 2
    lo_acc = x1_ref[:, :half]
    hi_acc = x1_ref[:, half:]
    for k in range(TOP_K):
        lo, hi = _unpack_bf16_pair(buf[k])
        gk = gate_ref[:, k:k + 1]
        lo_acc = lo_acc + gk * lo
        hi_acc = hi_acc + gk * hi
    o_ref[:, :half] = lo_acc
    o_ref[:, half:] = hi_acc


def _combine(dest, gates, x1, y, tm, tag):
    t = x1.shape[0]
    return pl.pallas_call(
        functools.partial(_combine_kernel, tm=tm),
        grid=(t // tm,),
        in_specs=[pl.BlockSpec((1, 1, tm * TOP_K), lambda i: (i, 0, 0), memory_space=pltpu.SMEM),
                  pl.BlockSpec((tm, LANES), lambda i: (i, 0)),
                  pl.BlockSpec((tm, D_MODEL), lambda i: (i, 0)),
                  pl.BlockSpec(memory_space=pl.ANY)],
        out_specs=pl.BlockSpec((tm, D_MODEL), lambda i: (i, 0)),
        out_shape=jax.ShapeDtypeStruct((t, D_MODEL), F32),
        scratch_shapes=[pltpu.VMEM((TOP_K, tm, D_MODEL // 2), U32), pltpu.SemaphoreType.DMA],
        compiler_params=_params(("arbitrary",), 32),
        name="combine_" + tag,
    )(dest.reshape(t // tm, 1, tm * TOP_K), gates, x1, y)


def _kv_state(k, v, g, batch, seq, keep):
    hs = slice(g * HEADS_PER_GROUP, (g + 1) * HEADS_PER_GROUP)
    k4 = k.reshape(batch, seq, B_HEADS, HEAD_DIM)[:, seq - keep:, hs]
    v4 = v.reshape(batch, seq, B_HEADS, HEAD_DIM)[:, seq - keep:, hs]
    return jnp.stack([k4, v4], axis=2)[None]


def kernel(x_prompt, x_sample, cache_kv_w128, cache_kv_w512, cache_kv_w2048, norm1, w_in, b_in_gate, gmlp_ln_g,
           gmlp_ln_b, gmlp_w_s, gmlp_b_s, q_gain, k_gain, w_a_out, w_b_out, w_o, norm2, router_w, router_b,
           exp_w_gate, exp_b_gate, exp_w_up, exp_b_up, exp_w_down, exp_b_down):
    assert norm1.shape[0] == 1, "single trunk layer"
    bp, sp, _ = x_prompt.shape
    bs, ss, _ = x_sample.shape
    tp, ts = bp * sp, bs * ss
    t_all = tp + ts
    xp = x_prompt.reshape(tp, D_MODEL)
    xs_ = x_sample.reshape(ts, D_MODEL)
    w_in2 = w_in[0]

    wa = _to_bf16(w_a_out[0])
    wb = _to_bf16(w_b_out[0])
    wo = _to_bf16(w_o[0])

    uv_p, q_p, k_p, v_p, gates_p = _token_mixer_inputs(xp, norm1[0], w_in2, b_in_gate[0], q_gain[0], k_gain[0],
                                                       1024, "p")
    a_p = _gmlp(uv_p, gmlp_ln_g[0], gmlp_ln_b[0], gmlp_w_s[0], gmlp_b_s[0].T, False, "p")
    o_parts, l_parts = [], []
    for g in range(len(B_PATTERNS)):
        o, lse = _attn_prompt(q_p, k_p, v_p, g, bp, sp)
        o_parts.append(o)
        l_parts.append(lse)
    m_p = _mix(a_p, o_parts + l_parts, gates_p, wa, wb, 512, "p")

    uv_s, q_s, k_s, v_s, gates_s = _token_mixer_inputs(xs_, norm1[0], w_in2, b_in_gate[0], q_gain[0], k_gain[0],
                                                       ts, "s")
    rep = CHUNK // ss
    w_mix_s = jnp.tile(gmlp_w_s[0][:, :ss, :ss], (1, rep, rep))
    bias_s = jnp.tile(gmlp_b_s[0][:, :ss].T, (rep, 1))
    a_s, vn_s = _gmlp(uv_s, gmlp_ln_g[0], gmlp_ln_b[0], w_mix_s, bias_s, True, "s")
    kvw = 2 * B_OUT
    b_s = _attn_sample(q_s, k_s, v_s,
                       cache_kv_w128[0].reshape(bs, -1, kvw), cache_kv_w512[0].reshape(bs, -1, kvw),
                       cache_kv_w2048[0].reshape(bs, -1, kvw), bs, ss)
    m_s = _mix(a_s, [b_s.reshape(ts, B_OUT)], gates_s, wa, wb, ts, "s")

    h2_buf = jnp.zeros((t_all, D_MODEL // 2), U32)
    lg_buf = jnp.zeros((t_all, N_EXPERTS), F32)
    x1_p, h2_buf, lg_buf = _out_proj(xp, m_p, wo, norm2[0], router_w[0], router_b[0], h2_buf, lg_buf, 0, 512, "p")
    x1_s, h2_buf, lg_buf = _out_proj(xs_, m_s, wo, norm2[0], router_w[0], router_b[0], h2_buf, lg_buf, tp, ts, "s")

    idx_tile, gate_tile, counts = _route(lg_buf)
    top_e = idx_tile[:, :TOP_K]
    rank = idx_tile[:, TOP_K:2 * TOP_K]
    counts = counts[0].astype(I32)
    padded = (counts + MOE_SUPER - 1) // MOE_SUPER * MOE_SUPER
    pad_end = jnp.cumsum(padded)
    pad_start = pad_end - padded
    dest = (pad_start[top_e] + rank).astype(I32)
    n_super = (t_all * TOP_K + N_EXPERTS * (MOE_SUPER - 1)) // MOE_SUPER
    sb_start = jnp.arange(n_super, dtype=I32) * MOE_SUPER
    sb_valid = sb_start < pad_end[-1]
    sb_e = jnp.minimum(jnp.searchsorted(pad_end, sb_start, side="right"), N_EXPERTS - 1).astype(I32)
    rem = counts[sb_e] - (sb_start - pad_start[sb_e])
    sb_nv = jnp.where(sb_valid, jnp.clip((rem + MOE_SUB - 1) // MOE_SUB, 0, MOE_SUPER // MOE_SUB), 0).astype(I32)
    last = pad_end[-1] // MOE_SUPER - 1
    sb_x = jnp.where(sb_valid, jnp.arange(n_super, dtype=I32), last).astype(I32)
    sb_e = jnp.where(sb_valid, sb_e, sb_e[last]).astype(I32)

    xs_sorted = _dispatch(h2_buf, dest, n_super * MOE_SUPER)
    y_sorted = _moe(xs_sorted, sb_e, sb_x, sb_nv, exp_w_gate[0], exp_b_gate[0], exp_w_up[0], exp_b_up[0],
                    exp_w_down[0], exp_b_down[0])
    y_p = _combine(dest[:tp], gate_tile[:tp], x1_p, y_sorted, 256, "p")
    y_s = _combine(dest[tp:], gate_tile[tp:], x1_s, y_sorted, ts, "s")

    keep = [min(w, sp) for w, _ in B_PATTERNS]
    return (y_p.reshape(bp, sp, D_MODEL),
            y_s.reshape(bs, ss, D_MODEL),
            _kv_state(k_p, v_p, 0, bp, sp, keep[0]),
            _kv_state(k_p, v_p, 1, bp, sp, keep[1]),
            _kv_state(k_p, v_p, 2, bp, sp, keep[2]),
            _kv_state(k_s, v_s, 0, bs, ss, ss),
            _kv_state(k_s, v_s, 1, bs, ss, ss),
            _kv_state(k_s, v_s, 2, bs, ss, ss),
            vn_s.reshape(1, bs, ss, A_WIDTH))
```

```python
import functools

import jax
import jax.numpy as jnp
from jax import lax
from jax.experimental import pallas as pl
from jax.experimental.pallas import tpu as pltpu

F32 = jnp.float32
BF16 = jnp.bfloat16
I32 = jnp.int32

D_MODEL = 2048
CHUNK = 128
A_GROUPS = 8
A_WIDTH = 1024
HEAD_DIM = 128
B_PATTERNS = ((128, 1), (512, 4), (2048, 16))
HEADS_PER_GROUP = 4
B_HEADS = len(B_PATTERNS) * HEADS_PER_GROUP
B_QKV = B_HEADS * HEAD_DIM
B_OUT = HEADS_PER_GROUP * HEAD_DIM
ATTN_SCALE = HEAD_DIM ** -0.5
N_EXPERTS = 32
TOP_K = 4
D_FF = 2048
SWIGLU_LIMIT = 7.0
SWIGLU_ALPHA = 1.702
EPS = 1e-6
SQRT_HALF = 0.7071067811865476

COL_UV = 0
COL_Q = 2 * A_WIDTH
COL_K = COL_Q + B_QKV
COL_V = COL_K + B_QKV
COL_GATE = COL_V + B_QKV

LANES = 128
MOE_SUPER = 1024
MOE_SUB = 256
MOE_TF = 256
MIB = 1 << 20


def _params(semantics, vmem_mib):
    return pltpu.CompilerParams(dimension_semantics=semantics, vmem_limit_bytes=vmem_mib * MIB)


def _rmsnorm_kernel(x_ref, g_ref, o_ref):
    x = x_ref[...]
    y = x * lax.rsqrt(jnp.mean(x * x, axis=-1, keepdims=True) + EPS)
    o_ref[...] = (y * g_ref[...]).astype(o_ref.dtype)


def _rmsnorm_bf16(x, g, tm):
    t, d = x.shape
    return pl.pallas_call(
        _rmsnorm_kernel,
        grid=(t // tm,),
        in_specs=[pl.BlockSpec((tm, d), lambda i: (i, 0)), pl.BlockSpec((1, d), lambda i: (0, 0))],
        out_specs=pl.BlockSpec((tm, d), lambda i: (i, 0)),
        out_shape=jax.ShapeDtypeStruct((t, d), BF16),
        compiler_params=_params(("parallel",), 32),
        name="rmsnorm1",
    )(x, g.reshape(1, d))


def _cast_kernel(w_ref, o_ref):
    o_ref[...] = w_ref[...].astype(o_ref.dtype)


def _to_bf16(w, tr=256):
    r, c = w.shape
    return pl.pallas_call(
        _cast_kernel,
        grid=(r // tr,),
        in_specs=[pl.BlockSpec((tr, c), lambda i: (i, 0))],
        out_specs=pl.BlockSpec((tr, c), lambda i: (i, 0)),
        out_shape=jax.ShapeDtypeStruct((r, c), BF16),
        compiler_params=_params(("parallel",), 32),
        name="cast_bf16",
    )(w)


def _ep_gelu(acc):
    return 0.5 * acc * (1.0 + lax.erf(acc * SQRT_HALF))


def _ep_identity(acc):
    return acc


def _ep_headnorm(acc, gain_ref):
    outs = []
    for h in range(acc.shape[1] // HEAD_DIM):
        a = acc[:, h * HEAD_DIM:(h + 1) * HEAD_DIM]
        ms = jnp.mean(a * a, axis=-1, keepdims=True)
        outs.append(a * lax.rsqrt(ms + EPS) * gain_ref[...])
    return jnp.concatenate(outs, axis=1)


def _ep_gate(acc, bias_ref):
    return jax.nn.sigmoid(acc + bias_ref[...])


def _proj_kernel(h_ref, w_ref, *rest, epilogue, n_extra):
    extra = rest[:n_extra]
    o_ref = rest[n_extra]
    wb_ref = rest[n_extra + 1]

    @pl.when(pl.program_id(1) == 0)
    def _():
        wb_ref[...] = w_ref[...].astype(BF16)

    tm = h_ref.shape[0]
    step = min(tm, 256)
    for r0 in range(0, tm, step):
        acc = jnp.dot(h_ref[r0:r0 + step, :], wb_ref[...], preferred_element_type=F32)
        o_ref[r0:r0 + step, :] = epilogue(acc, *extra).astype(o_ref.dtype)


def _in_proj(h, w, col0, ncols, epilogue, extras, extra_specs, out_dtype, tm, name, tn=512):
    t, k = h.shape
    j0 = col0 // tn
    return pl.pallas_call(
        functools.partial(_proj_kernel, epilogue=epilogue, n_extra=len(extras)),
        grid=(ncols // tn, t // tm),
        in_specs=[pl.BlockSpec((tm, k), lambda j, i: (i, 0)),
                  pl.BlockSpec((k, tn), lambda j, i: (0, j0 + j))] + extra_specs,
        out_specs=pl.BlockSpec((tm, tn), lambda j, i: (i, j)),
        out_shape=jax.ShapeDtypeStruct((t, ncols), out_dtype),
        scratch_shapes=[pltpu.VMEM((k, tn), BF16)],
        compiler_params=_params(("arbitrary", "arbitrary"), 48),
        name=name,
    )(h, w, *extras)


def _token_mixer_inputs(x, norm1, w_in, b_in_gate, q_gain, k_gain, tm, tag):
    h = _rmsnorm_bf16(x, norm1, min(tm, 512))
    gain_spec = [pl.BlockSpec((1, HEAD_DIM), lambda j, i: (0, 0))]
    uv = _in_proj(h, w_in, COL_UV, 2 * A_WIDTH, _ep_gelu, [], [], BF16, tm, "proj_uv_" + tag)
    q = _in_proj(h, w_in, COL_Q, B_QKV, _ep_headnorm, [q_gain.reshape(1, HEAD_DIM)], gain_spec, F32, tm,
                 "proj_q_" + tag)
    k = _in_proj(h, w_in, COL_K, B_QKV, _ep_headnorm, [k_gain.reshape(1, HEAD_DIM)], gain_spec, F32, tm,
                 "proj_k_" + tag)
    v = _in_proj(h, w_in, COL_V, B_QKV, _ep_identity, [], [], F32, tm, "proj_v_" + tag)
    gates = _in_proj(h, w_in, COL_GATE, 2 * D_MODEL, _ep_gate, [b_in_gate.reshape(1, 2 * D_MODEL)],
                     [pl.BlockSpec((1, 512), lambda j, i: (0, j))], BF16, tm, "proj_gate_" + tag)
    return uv, q, k, v, gates


def _gmlp_kernel(uv_ref, lng_ref, lnb_ref, w_ref, bias_ref, a_ref, *vn_out, sample, rows):
    v = uv_ref[:, A_WIDTH:].astype(F32)
    xc = v - jnp.mean(v, axis=-1, keepdims=True)
    vn = xc * lax.rsqrt(jnp.mean(xc * xc, axis=-1, keepdims=True) + EPS) * lng_ref[...] + lnb_ref[...]
    if vn_out:
        vn_out[0][...] = vn
    vnb = vn.astype(BF16)
    row = lax.broadcasted_iota(I32, (CHUNK, CHUNK), 0)
    col = lax.broadcasted_iota(I32, (CHUNK, CHUNK), 1)
    if sample:
        mask = ((row >> 3) == (col >> 3)) & ((row & 7) >= (col & 7))
    else:
        mask = row >= col
    for g in range(A_GROUPS):
        wm = jnp.where(mask, w_ref[g], 0.0).astype(BF16)
        bcol = bias_ref[:, g:g + 1]
        cs = slice(g * CHUNK, (g + 1) * CHUNK)
        for c in range(rows // CHUNK):
            rs = slice(c * CHUNK, (c + 1) * CHUNK)
            s = jnp.dot(wm, vnb[rs, cs], preferred_element_type=F32) + bcol
            a_ref[rs, cs] = (uv_ref[rs, cs].astype(F32) * s).astype(a_ref.dtype)


def _gmlp(uv, ln_g, ln_b, w_mix, bias_t, sample, tag, rows=256):
    t = uv.shape[0]
    out_shape = [jax.ShapeDtypeStruct((t, A_WIDTH), BF16)]
    out_specs = [pl.BlockSpec((rows, A_WIDTH), lambda i: (i, 0))]
    if sample:
        out_shape.append(jax.ShapeDtypeStruct((t, A_WIDTH), F32))
        out_specs.append(pl.BlockSpec((rows, A_WIDTH), lambda i: (i, 0)))
    res = pl.pallas_call(
        functools.partial(_gmlp_kernel, sample=sample, rows=rows),
        grid=(t // rows,),
        in_specs=[pl.BlockSpec((rows, 2 * A_WIDTH), lambda i: (i, 0)),
                  pl.BlockSpec((1, A_WIDTH), lambda i: (0, 0)),
                  pl.BlockSpec((1, A_WIDTH), lambda i: (0, 0)),
                  pl.BlockSpec((A_GROUPS, CHUNK, CHUNK), lambda i: (0, 0, 0)),
                  pl.BlockSpec((CHUNK, A_GROUPS), lambda i: (0, 0))],
        out_specs=out_specs,
        out_shape=out_shape,
        compiler_params=_params(("parallel",), 32),
        name="gmlp_" + tag,
    )(uv, ln_g.reshape(1, A_WIDTH), ln_b.reshape(1, A_WIDTH), w_mix, bias_t)
    return res if sample else res[0]


def _attn_prompt_kernel(*refs, dil):
    nh = HEADS_PER_GROUP
    q_refs, k_refs, v_refs = refs[:nh], refs[nh:2 * nh], refs[2 * nh:3 * nh]
    o_ref, lse_ref, kprev, vprev, obuf = refs[3 * nh:]
    n = pl.program_id(1)
    L = CHUNK

    @pl.when(n == 0)
    def _():
        kprev[...] = jnp.zeros_like(kprev)
        vprev[...] = jnp.zeros_like(vprev)

    qi = lax.broadcasted_iota(I32, (L, 2 * L), 0)
    ki = lax.broadcasted_iota(I32, (L, 2 * L), 1)
    dist = qi + L - ki
    first_key = jnp.where(n > 0, 0, L)
    valid = (dist >= 0) & (dist <= L) & (ki >= first_key)
    lane = lax.broadcasted_iota(I32, (L, LANES), 1)

    def residue(r, carry):
        rows = pl.ds(r, L, stride=dil) if dil > 1 else pl.ds(0, L)
        lse_tile = jnp.zeros((L, LANES), F32)
        for h in range(nh):
            cs = slice(h * HEAD_DIM, (h + 1) * HEAD_DIM)
            qr = q_refs[h][0, rows, :].astype(BF16)
            kr = k_refs[h][0, rows, :].astype(BF16)
            vr = v_refs[h][0, rows, :].astype(BF16)
            kcat = jnp.concatenate([kprev[r, :, cs], kr], axis=0)
            vcat = jnp.concatenate([vprev[r, :, cs], vr], axis=0)
            kprev[r, :, cs] = kr
            vprev[r, :, cs] = vr
            s = lax.dot_general(qr, kcat, (((1,), (1,)), ((), ())), preferred_element_type=F32) * ATTN_SCALE
            s = jnp.where(valid, s, -jnp.inf)
            m = jnp.max(s, axis=-1, keepdims=True)
            p = jnp.exp(s - m)
            den = jnp.sum(p, axis=-1, keepdims=True)
            obuf[h, rows, :] = jnp.dot(p.astype(BF16), vcat, preferred_element_type=F32) / den
            lse_tile = jnp.where(lane == h, m + jnp.log(den), lse_tile)
        lse_ref[0, rows, :] = lse_tile
        return carry

    if dil == 1:
        residue(0, 0)
    else:
        lax.fori_loop(0, dil, residue, 0)
    for h in range(nh):
        o_ref[0, :, h * HEAD_DIM:(h + 1) * HEAD_DIM] = obuf[h]


def _attn_prompt(q, k, v, g, batch, seq):
    _, dil = B_PATTERNS[g]
    rows = CHUNK * dil
    nh = HEADS_PER_GROUP
    head_specs = [pl.BlockSpec((1, rows, HEAD_DIM), functools.partial(lambda b, n, c: (b, n, c), c=g * nh + h))
                  for h in range(nh)]
    out_map = lambda b, n: (b, n, 0)
    q3, k3, v3 = (t.reshape(batch, seq, B_QKV) for t in (q, k, v))
    o, lse = pl.pallas_call(
        functools.partial(_attn_prompt_kernel, dil=dil),
        grid=(batch, seq // rows),
        in_specs=head_specs * 3,
        out_specs=[pl.BlockSpec((1, rows, B_OUT), out_map), pl.BlockSpec((1, rows, LANES), out_map)],
        out_shape=[jax.ShapeDtypeStruct((batch, seq, B_OUT), F32),
                   jax.ShapeDtypeStruct((batch, seq, LANES), F32)],
        scratch_shapes=[pltpu.VMEM((dil, CHUNK, B_OUT), BF16), pltpu.VMEM((dil, CHUNK, B_OUT), BF16),
                        pltpu.VMEM((nh, rows, HEAD_DIM), F32)],
        compiler_params=_params(("parallel", "arbitrary"), 48),
        name="attn_prompt_g%d" % g,
    )(*([q3] * nh + [k3] * nh + [v3] * nh))
    return o.reshape(batch * seq, B_OUT), lse.reshape(batch * seq, LANES)


def _attn_sample_kernel(q_ref, k_ref, v_ref, c1_ref, c2_ref, c3_ref, b_ref):
    S = q_ref.shape[1]
    nh = HEADS_PER_GROUP
    for s in range(S):
        outs, lses = [], []
        for g, (win, dil) in enumerate(B_PATTERNS):
            hs = slice(g * nh, (g + 1) * nh)
            cref = (c1_ref, c2_ref, c3_ref)[g]
            m0 = s // dil
            new_rows = list(range(s % dil, s + 1, dil))
            kparts = [cref[0, 0, m0:, s % dil, 0]] + [k_ref[0, j:j + 1, hs, :] for j in new_rows]
            vparts = [cref[0, 0, m0:, s % dil, 1]] + [v_ref[0, j:j + 1, hs, :] for j in new_rows]
            kk = jnp.concatenate(kparts, axis=0)
            vv = jnp.concatenate(vparts, axis=0)
            sc = jnp.sum(kk * q_ref[0, s:s + 1, hs, :], axis=-1, keepdims=True) * ATTN_SCALE
            m = jnp.max(sc, axis=0, keepdims=True)
            p = jnp.exp(sc - m)
            den = jnp.sum(p, axis=0, keepdims=True)
            outs.append(jnp.sum(p * vv, axis=0, keepdims=True) / den)
            lses.append(m + jnp.log(den))
        mx = jnp.maximum(jnp.maximum(lses[0], lses[1]), lses[2])
        e = [jnp.exp(l - mx) for l in lses]
        den = e[0] + e[1] + e[2]
        b_ref[0, s:s + 1] = (e[0] / den) * outs[0] + (e[1] / den) * outs[1] + (e[2] / den) * outs[2]


def _attn_sample(q, k, v, caches, batch, seq):
    nh = HEADS_PER_GROUP
    assert seq <= CHUNK
    specs, views = [], []
    for c, (win, dil) in zip(caches, B_PATTERNS):
        assert c.shape[0] == 1 and c.shape[2] == win, "decode cache must hold exactly the pattern's window"
        nres = min(dil, seq)
        views.append(c.reshape(batch, 1, win // dil, dil, 2, nh, HEAD_DIM))
        specs.append(pl.BlockSpec((1, 1, win // dil, nres, 2, nh, HEAD_DIM), lambda b: (b, 0, 0, 0, 0, 0, 0)))
    qkv_spec = pl.BlockSpec((1, seq, B_HEADS, HEAD_DIM), lambda b: (b, 0, 0, 0))
    q4, k4, v4 = (t.reshape(batch, seq, B_HEADS, HEAD_DIM) for t in (q, k, v))
    return pl.pallas_call(
        _attn_sample_kernel,
        grid=(batch,),
        in_specs=[qkv_spec] * 3 + specs,
        out_specs=pl.BlockSpec((1, seq, nh, HEAD_DIM), lambda b: (b, 0, 0, 0)),
        out_shape=jax.ShapeDtypeStruct((batch, seq, nh, HEAD_DIM), F32),
        compiler_params=_params(("parallel",), 40),
        name="attn_sample",
    )(q4, k4, v4, *views)


def _combine_groups(o_refs, l_refs):
    comb = []
    for h in range(HEADS_PER_GROUP):
        cs = slice(h * HEAD_DIM, (h + 1) * HEAD_DIM)
        ls = [l[:, h:h + 1] for l in l_refs]
        mx = jnp.maximum(jnp.maximum(ls[0], ls[1]), ls[2])
        e = [jnp.exp(l - mx) for l in ls]
        den = e[0] + e[1] + e[2]
        comb.append(sum((e[g] / den) * o_refs[g][:, cs] for g in range(3)))
    return jnp.concatenate(comb, axis=1).astype(BF16)


def _mix_kernel(*refs, combine):
    if combine:
        a_ref = refs[0]
        b = _combine_groups(refs[1:4], refs[4:7])
        ga_ref, gb_ref, wa_ref, wb_ref, m_ref = refs[7:]
    else:
        a_ref, b_ref, ga_ref, gb_ref, wa_ref, wb_ref, m_ref = refs
        b = b_ref[...].astype(BF16)
    ap = jnp.dot(a_ref[...], wa_ref[...], preferred_element_type=F32)
    bp = jnp.dot(b, wb_ref[...], preferred_element_type=F32)
    m_ref[...] = (ga_ref[...].astype(F32) * ap + gb_ref[...].astype(F32) * bp).astype(m_ref.dtype)


def _mix(a, b_parts, gates, wa, wb, tm, tag):
    t = a.shape[0]
    combine = len(b_parts) > 1
    row = lambda w: pl.BlockSpec((tm, w), lambda i: (i, 0))
    if combine:
        b_specs = [row(B_OUT)] * 3 + [row(LANES)] * 3
    else:
        b_specs = [row(B_OUT)]
    return pl.pallas_call(
        functools.partial(_mix_kernel, combine=combine),
        grid=(t // tm,),
        in_specs=[row(A_WIDTH)] + b_specs + [
            pl.BlockSpec((tm, D_MODEL), lambda i: (i, 0)),
            pl.BlockSpec((tm, D_MODEL), lambda i: (i, 1)),
            pl.BlockSpec((A_WIDTH, D_MODEL), lambda i: (0, 0)),
            pl.BlockSpec((B_OUT, D_MODEL), lambda i: (0, 0))],
        out_specs=row(D_MODEL),
        out_shape=jax.ShapeDtypeStruct((t, D_MODEL), BF16),
        compiler_params=_params(("parallel",), 48),
        name="mix_" + tag,
    )(a, *b_parts, gates, gates, wa, wb)


def _out_kernel(xp_ref, xs_ref, mp_ref, ms_ref, wo_ref, g2_ref, wr_ref, rb_ref, x1_ref, h2_ref, lg_ref, *, n_prompt):
    is_prompt = pl.program_id(0) < n_prompt
    x = jnp.where(is_prompt, xp_ref[...], xs_ref[...])
    m = jnp.where(is_prompt, mp_ref[...], ms_ref[...])
    x1 = x + jnp.dot(m, wo_ref[...], preferred_element_type=F32)
    x1_ref[...] = x1
    h = x1 * lax.rsqrt(jnp.mean(x1 * x1, axis=-1, keepdims=True) + EPS) * g2_ref[...]
    h_hi = h.astype(BF16)
    h_lo = (h - h_hi.astype(F32)).astype(BF16)
    w = wr_ref[...]
    w_hi = w.astype(BF16)
    w_lo = (w - w_hi.astype(F32)).astype(BF16)
    lg = jnp.dot(h_hi, w_hi, preferred_element_type=F32)
    lg += jnp.dot(h_lo, w_hi, preferred_element_type=F32)
    lg += jnp.dot(h_hi, w_lo, preferred_element_type=F32)
    lg_ref[...] = lg + rb_ref[...]
    h2_ref[...] = h


def _out_proj(xp, xs, mp, ms, wo, norm2, router_w, router_b):
    tp, ts = xp.shape[0], xs.shape[0]
    tm = ts
    assert tp % tm == 0
    n_prompt = tp // tm
    t_all = tp + ts
    const = lambda r, c: pl.BlockSpec((r, c), lambda i: (0, 0))
    p_spec = pl.BlockSpec((tm, D_MODEL), lambda i: (jnp.minimum(i, n_prompt - 1), 0))
    s_spec = pl.BlockSpec((tm, D_MODEL), lambda i: (0, 0))
    return pl.pallas_call(
        functools.partial(_out_kernel, n_prompt=n_prompt),
        grid=(n_prompt + 1,),
        in_specs=[p_spec, s_spec, p_spec, s_spec,
                  const(D_MODEL, D_MODEL), const(1, D_MODEL), const(D_MODEL, N_EXPERTS), const(1, N_EXPERTS)],
        out_specs=[pl.BlockSpec((tm, D_MODEL), lambda i: (i, 0)),
                   pl.BlockSpec((tm, D_MODEL), lambda i: (i, 0)),
                   pl.BlockSpec((tm, N_EXPERTS), lambda i: (i, 0))],
        out_shape=[jax.ShapeDtypeStruct((t_all, D_MODEL), F32),
                   jax.ShapeDtypeStruct((t_all, D_MODEL), F32),
                   jax.ShapeDtypeStruct((t_all, N_EXPERTS), F32)],
        compiler_params=_params(("arbitrary",), 48),
        name="out_proj",
    )(xp, xs, mp, ms, wo, norm2.reshape(1, D_MODEL), router_w, router_b.reshape(1, N_EXPERTS))


def _route_kernel(lg_ref, idx_ref, gate_ref, cnt_ref, *, rows):
    @pl.when(pl.program_id(0) == 0)
    def _():
        cnt_ref[...] = jnp.zeros_like(cnt_ref)

    l = lg_ref[...]
    lane = lax.broadcasted_iota(I32, l.shape, 1)
    vals, idxs = [], []
    for _ in range(TOP_K):
        m = jnp.max(l, axis=-1, keepdims=True)
        idx = jnp.min(jnp.where(l == m, lane, N_EXPERTS), axis=-1, keepdims=True)
        vals.append(m)
        idxs.append(idx)
        l = jnp.where(lane == idx, -jnp.inf, l)
    es = [jnp.exp(v - vals[0]) for v in vals]
    den = es[0] + es[1] + es[2] + es[3]
    onehot = sum((lane == idx).astype(F32) for idx in idxs)
    r = lax.broadcasted_iota(I32, (rows, rows), 0)
    c = lax.broadcasted_iota(I32, (rows, rows), 1)
    tri = (r > c).astype(BF16)
    before = jnp.dot(tri, onehot.astype(BF16), preferred_element_type=F32) + cnt_ref[...]
    cnt_ref[...] += jnp.sum(onehot, axis=0, keepdims=True)
    wide = lax.broadcasted_iota(I32, (rows, LANES), 1)
    idx_tile = jnp.zeros((rows, LANES), I32)
    gate_tile = jnp.zeros((rows, LANES), F32)
    for k in range(TOP_K):
        rank = jnp.sum(jnp.where(lane == idxs[k], before, 0.0), axis=-1, keepdims=True).astype(I32)
        idx_tile = jnp.where(wide == k, idxs[k], idx_tile)
        idx_tile = jnp.where(wide == TOP_K + k, rank, idx_tile)
        gate_tile = jnp.where(wide == k, es[k] / den, gate_tile)
    idx_ref[...] = idx_tile
    gate_ref[...] = gate_tile


def _route(logits, rows=256):
    t = logits.shape[0]
    return pl.pallas_call(
        functools.partial(_route_kernel, rows=rows),
        grid=(t // rows,),
        in_specs=[pl.BlockSpec((rows, N_EXPERTS), lambda i: (i, 0))],
        out_specs=[pl.BlockSpec((rows, LANES), lambda i: (i, 0)),
                   pl.BlockSpec((rows, LANES), lambda i: (i, 0)),
                   pl.BlockSpec((1, N_EXPERTS), lambda i: (0, 0))],
        out_shape=[jax.ShapeDtypeStruct((t, LANES), I32),
                   jax.ShapeDtypeStruct((t, LANES), F32),
                   jax.ShapeDtypeStruct((1, N_EXPERTS), F32)],
        compiler_params=_params(("arbitrary",), 32),
        name="route",
    )(logits)


def _dispatch_kernel(dest_ref, h_ref, xs_hbm, sem, *, tm):
    def body(t, carry):
        src = h_ref.at[pl.ds(t, 1)]
        for k in range(TOP_K):
            pltpu.make_async_copy(src, xs_hbm.at[pl.ds(dest_ref[0, 0, t * TOP_K + k], 1)], sem).start()
        return carry

    lax.fori_loop(0, tm, body, 0)
    for k in range(TOP_K):
        pltpu.make_async_copy(h_ref, xs_hbm.at[pl.ds(0, tm)], sem).wait()


def _dispatch(h2, dest, n_slots, tm=1280):
    t, w = h2.shape
    assert t % tm == 0
    return pl.pallas_call(
        functools.partial(_dispatch_kernel, tm=tm),
        grid=(t // tm,),
        in_specs=[pl.BlockSpec((1, 1, tm * TOP_K), lambda i: (i, 0, 0), memory_space=pltpu.SMEM),
                  pl.BlockSpec((tm, w), lambda i: (i, 0))],
        out_specs=pl.BlockSpec(memory_space=pl.ANY),
        out_shape=jax.ShapeDtypeStruct((n_slots, w), h2.dtype),
        scratch_shapes=[pltpu.SemaphoreType.DMA],
        compiler_params=_params(("arbitrary",), 32),
        name="dispatch",
    )(dest.reshape(t // tm, 1, tm * TOP_K), h2)


def _moe_kernel(be_ref, bx_ref, nv_ref, x_ref, wg_ref, wu_ref, wd_ref, bg_ref, bu_ref, bd_ref, o_ref, xb_ref):
    del be_ref, bx_ref
    f = pl.program_id(1)
    nv = nv_ref[pl.program_id(0)]

    @pl.when((f == 0) & (nv > 0))
    def _():
        o_ref[...] = jnp.broadcast_to(bd_ref[0], o_ref.shape)

    def ffn(nrows):
        rows = pl.ds(0, nrows)

        @pl.when(f == 0)
        def _():
            xb_ref[rows, :] = x_ref[rows, :].astype(BF16)

        xb = xb_ref[rows, :]
        g = jnp.dot(xb, wg_ref[0].astype(BF16), preferred_element_type=F32) + bg_ref[0]
        u = jnp.dot(xb, wu_ref[0].astype(BF16), preferred_element_type=F32) + bu_ref[0]
        g = jnp.minimum(g, SWIGLU_LIMIT)
        u = jnp.clip(u, -SWIGLU_LIMIT, SWIGLU_LIMIT)
        act = ((u + 1.0) * (g * jax.nn.sigmoid(SWIGLU_ALPHA * g))).astype(BF16)
        o_ref[rows, :] += jnp.dot(act, wd_ref[0].astype(BF16), preferred_element_type=F32)

    for j in range(1, MOE_SUPER // MOE_SUB + 1):
        @pl.when(nv == j)
        def _():
            ffn(j * MOE_SUB)


def _moe(xs, blk_e, blk_x, blk_nv, w_gate, b_gate, w_up, b_up, w_down, b_down):
    n_super = xs.shape[0] // MOE_SUPER
    n_f = D_FF // MOE_TF
    last_f = n_f - 1

    def f_of(m, f, nv):
        return jnp.where(nv[m] > 0, f, last_f)

    grid_spec = pltpu.PrefetchScalarGridSpec(
        num_scalar_prefetch=3,
        grid=(n_super, n_f),
        in_specs=[
            pl.BlockSpec((MOE_SUPER, D_MODEL), lambda m, f, be, bx, nv: (bx[m], 0)),
            pl.BlockSpec((1, D_MODEL, MOE_TF), lambda m, f, be, bx, nv: (be[m], 0, f_of(m, f, nv))),
            pl.BlockSpec((1, D_MODEL, MOE_TF), lambda m, f, be, bx, nv: (be[m], 0, f_of(m, f, nv))),
            pl.BlockSpec((1, MOE_TF, D_MODEL), lambda m, f, be, bx, nv: (be[m], f_of(m, f, nv), 0)),
            pl.BlockSpec((1, 1, MOE_TF), lambda m, f, be, bx, nv: (be[m], 0, f_of(m, f, nv))),
            pl.BlockSpec((1, 1, MOE_TF), lambda m, f, be, bx, nv: (be[m], 0, f_of(m, f, nv))),
            pl.BlockSpec((1, 1, D_MODEL), lambda m, f, be, bx, nv: (be[m], 0, 0)),
        ],
        out_specs=pl.BlockSpec((MOE_SUPER, D_MODEL), lambda m, f, be, bx, nv: (bx[m], 0)),
        scratch_shapes=[pltpu.VMEM((MOE_SUPER, D_MODEL), BF16)],
    )
    return pl.pallas_call(
        _moe_kernel,
        grid_spec=grid_spec,
        out_shape=jax.ShapeDtypeStruct(xs.shape, F32),
        compiler_params=_params(("arbitrary", "arbitrary"), 60),
        name="moe_ffn",
    )(blk_e, blk_x, blk_nv, xs, w_gate, w_up, w_down,
      b_gate.reshape(N_EXPERTS, 1, D_FF), b_up.reshape(N_EXPERTS, 1, D_FF), b_down.reshape(N_EXPERTS, 1, D_MODEL))


def _combine_kernel(dest_ref, gate_ref, x1_ref, y_hbm, o_ref, buf, sem, *, tm):
    def body(t, carry):
        for k in range(TOP_K):
            pltpu.make_async_copy(y_hbm.at[pl.ds(dest_ref[0, 0, t * TOP_K + k], 1)],
                                  buf.at[k, pl.ds(t, 1)], sem).start()
        return carry

    lax.fori_loop(0, tm, body, 0)
    for k in range(TOP_K):
        pltpu.make_async_copy(y_hbm.at[pl.ds(0, tm)], buf.at[k], sem).wait()
    acc = x1_ref[...]
    for k in range(TOP_K):
        acc = acc + gate_ref[:, k:k + 1] * buf[k]
    o_ref[...] = acc


def _combine(dest, gates, x1, y, row0, t, tm, tag):
    t_all = x1.shape[0]
    blk0 = row0 // tm
    return pl.pallas_call(
        functools.partial(_combine_kernel, tm=tm),
        grid=(t // tm,),
        in_specs=[pl.BlockSpec((1, 1, tm * TOP_K), lambda i: (blk0 + i, 0, 0), memory_space=pltpu.SMEM),
                  pl.BlockSpec((tm, LANES), lambda i: (blk0 + i, 0)),
                  pl.BlockSpec((tm, D_MODEL), lambda i: (blk0 + i, 0)),
                  pl.BlockSpec(memory_space=pl.ANY)],
        out_specs=pl.BlockSpec((tm, D_MODEL), lambda i: (i, 0)),
        out_shape=jax.ShapeDtypeStruct((t, D_MODEL), F32),
        scratch_shapes=[pltpu.VMEM((TOP_K, tm, D_MODEL), F32), pltpu.SemaphoreType.DMA],
        compiler_params=_params(("arbitrary",), 40),
        name="combine_" + tag,
    )(dest.reshape(t_all // tm, 1, tm * TOP_K), gates, x1, y)


def _kv_state(k, v, g, batch, seq, keep):
    cs = slice(g * B_OUT, (g + 1) * B_OUT)
    k4 = k.reshape(batch, seq, B_QKV)[:, seq - keep:, cs].reshape(batch, keep, HEADS_PER_GROUP, HEAD_DIM)
    v4 = v.reshape(batch, seq, B_QKV)[:, seq - keep:, cs].reshape(batch, keep, HEADS_PER_GROUP, HEAD_DIM)
    return jnp.stack([k4, v4], axis=2)[None]


def kernel(x_prompt, x_sample, cache_kv_w128, cache_kv_w512, cache_kv_w2048, norm1, w_in, b_in_gate, gmlp_ln_g,
           gmlp_ln_b, gmlp_w_s, gmlp_b_s, q_gain, k_gain, w_a_out, w_b_out, w_o, norm2, router_w, router_b,
           exp_w_gate, exp_b_gate, exp_w_up, exp_b_up, exp_w_down, exp_b_down):
    assert norm1.shape[0] == 1, "single trunk layer"
    bp, sp, _ = x_prompt.shape
    bs, ss, _ = x_sample.shape
    tp, ts = bp * sp, bs * ss
    t_all = tp + ts
    xp = x_prompt.reshape(tp, D_MODEL)
    xs_ = x_sample.reshape(ts, D_MODEL)
    w_in2 = w_in[0]

    wa = _to_bf16(w_a_out[0])
    wb = _to_bf16(w_b_out[0])
    wo = _to_bf16(w_o[0])

    uv_p, q_p, k_p, v_p, gates_p = _token_mixer_inputs(xp, norm1[0], w_in2, b_in_gate[0], q_gain[0], k_gain[0],
                                                       1024, "p")
    a_p = _gmlp(uv_p, gmlp_ln_g[0], gmlp_ln_b[0], gmlp_w_s[0], gmlp_b_s[0].T, False, "p")
    o_parts, l_parts = [], []
    for g in range(len(B_PATTERNS)):
        o, lse = _attn_prompt(q_p, k_p, v_p, g, bp, sp)
        o_parts.append(o)
        l_parts.append(lse)
    m_p = _mix(a_p, o_parts + l_parts, gates_p, wa, wb, 512, "p")

    uv_s, q_s, k_s, v_s, gates_s = _token_mixer_inputs(xs_, norm1[0], w_in2, b_in_gate[0], q_gain[0], k_gain[0],
                                                       ts, "s")
    rep = CHUNK // ss
    w_mix_s = jnp.tile(gmlp_w_s[0][:, :ss, :ss], (1, rep, rep))
    bias_s = jnp.tile(gmlp_b_s[0][:, :ss].T, (rep, 1))
    a_s, vn_s = _gmlp(uv_s, gmlp_ln_g[0], gmlp_ln_b[0], w_mix_s, bias_s, True, "s")
    b_s = _attn_sample(q_s, k_s, v_s, (cache_kv_w128, cache_kv_w512, cache_kv_w2048), bs, ss)
    m_s = _mix(a_s, [b_s.reshape(ts, B_OUT)], gates_s, wa, wb, ts, "s")

    x1, h2, logits = _out_proj(xp, xs_, m_p, m_s, wo, norm2[0], router_w[0], router_b[0])
    y_p, y_s = _moe_layer(x1, h2, logits, tp, ts, exp_w_gate[0], exp_b_gate[0], exp_w_up[0], exp_b_up[0],
                          exp_w_down[0], exp_b_down[0])

    keep = [min(w, sp) for w, _ in B_PATTERNS]
    return (y_p.reshape(bp, sp, D_MODEL),
            y_s.reshape(bs, ss, D_MODEL),
            _kv_state(k_p, v_p, 0, bp, sp, keep[0]),
            _kv_state(k_p, v_p, 1, bp, sp, keep[1]),
            _kv_state(k_p, v_p, 2, bp, sp, keep[2]),
            _kv_state(k_s, v_s, 0, bs, ss, ss),
            _kv_state(k_s, v_s, 1, bs, ss, ss),
            _kv_state(k_s, v_s, 2, bs, ss, ss),
            vn_s.reshape(1, bs, ss, A_WIDTH))


def _moe_layer(x1, h2, logits, tp, ts, w_gate, b_gate, w_up, b_up, w_down, b_down):
    t_all = tp + ts
    idx_tile, gate_tile, counts = _route(logits)
    top_e = idx_tile[:, :TOP_K]
    rank = idx_tile[:, TOP_K:2 * TOP_K]
    counts = counts[0].astype(I32)
    sub_per_blk = MOE_SUPER // MOE_SUB
    nsub = (counts + MOE_SUB - 1) // MOE_SUB
    nblk = (nsub + sub_per_blk - 1) // sub_per_blk
    base = nsub // jnp.maximum(nblk, 1)
    extra = nsub - base * nblk
    blk_end = jnp.cumsum(nblk)
    blk_start = blk_end - nblk
    sub_t = rank // MOE_SUB
    base_t, extra_t = jnp.maximum(base[top_e], 1), extra[top_e]
    in_big = sub_t < extra_t * (base_t + 1)
    rest = sub_t - extra_t * (base_t + 1)
    blk_t = jnp.where(in_big, sub_t // (base_t + 1), extra_t + rest // base_t)
    off_t = jnp.where(in_big, sub_t % (base_t + 1), rest % base_t)
    dest = ((blk_start[top_e] + blk_t) * MOE_SUPER + off_t * MOE_SUB + rank % MOE_SUB).astype(I32)
    n_super = (t_all * TOP_K + N_EXPERTS * (MOE_SUPER - 1)) // MOE_SUPER
    sb = jnp.arange(n_super, dtype=I32)
    sb_valid = sb < blk_end[-1]
    sb_e = jnp.minimum(jnp.sum((blk_end[None, :] <= sb[:, None]).astype(I32), axis=1), N_EXPERTS - 1)
    sb_nv = jnp.where(sb_valid, base[sb_e] + ((sb - blk_start[sb_e]) < extra[sb_e]).astype(I32), 0).astype(I32)
    last = blk_end[-1] - 1
    sb_x = jnp.where(sb_valid, sb, last).astype(I32)
    sb_e = jnp.where(sb_valid, sb_e, sb_e[last]).astype(I32)

    xs_sorted = _dispatch(h2, dest, n_super * MOE_SUPER)
    y_sorted = _moe(xs_sorted, sb_e, sb_x, sb_nv, w_gate, b_gate, w_up, b_up, w_down, b_down)
    y_p = _combine(dest, gate_tile, x1, y_sorted, 0, tp, 256, "p")
    y_s = _combine(dest, gate_tile, x1, y_sorted, tp, ts, ts, "s")
    return y_p, y_s
```

```python
import functools

import jax
import jax.numpy as jnp
from jax import lax
from jax.experimental import pallas as pl
from jax.experimental.pallas import tpu as pltpu

F32 = jnp.float32
BF16 = jnp.bfloat16
I32 = jnp.int32

D_MODEL = 2048
CHUNK = 128
A_GROUPS = 8
A_WIDTH = 1024
HEAD_DIM = 128
B_PATTERNS = ((128, 1), (512, 4), (2048, 16))
HEADS_PER_GROUP = 4
B_HEADS = len(B_PATTERNS) * HEADS_PER_GROUP
B_QKV = B_HEADS * HEAD_DIM
B_OUT = HEADS_PER_GROUP * HEAD_DIM
ATTN_SCALE = HEAD_DIM ** -0.5
N_EXPERTS = 32
TOP_K = 4
D_FF = 2048
SWIGLU_LIMIT = 7.0
SWIGLU_ALPHA = 1.702
EPS = 1e-6
SQRT_HALF = 0.7071067811865476

COL_UV = 0
COL_Q = 2 * A_WIDTH
COL_K = COL_Q + B_QKV
COL_V = COL_K + B_QKV
COL_GATE = COL_V + B_QKV

LANES = 128
MOE_SUPER = 768
MOE_SUB = 256
MOE_TF = 512
MIB = 1 << 20


def _params(semantics, vmem_mib):
    return pltpu.CompilerParams(dimension_semantics=semantics, vmem_limit_bytes=vmem_mib * MIB)


def _rmsnorm_kernel(x_ref, g_ref, o_ref):
    x = x_ref[...]
    y = x * lax.rsqrt(jnp.mean(x * x, axis=-1, keepdims=True) + EPS)
    o_ref[...] = (y * g_ref[...]).astype(o_ref.dtype)


def _rmsnorm_bf16(x, g, tm):
    t, d = x.shape
    return pl.pallas_call(
        _rmsnorm_kernel,
        grid=(t // tm,),
        in_specs=[pl.BlockSpec((tm, d), lambda i: (i, 0)), pl.BlockSpec((1, d), lambda i: (0, 0))],
        out_specs=pl.BlockSpec((tm, d), lambda i: (i, 0)),
        out_shape=jax.ShapeDtypeStruct((t, d), BF16),
        compiler_params=_params(("parallel",), 32),
        name="rmsnorm1",
    )(x, g.reshape(1, d))


def _cast_kernel(w_ref, o_ref):
    o_ref[...] = w_ref[...].astype(o_ref.dtype)


def _to_bf16(w, tr=256):
    r, c = w.shape
    return pl.pallas_call(
        _cast_kernel,
        grid=(r // tr,),
        in_specs=[pl.BlockSpec((tr, c), lambda i: (i, 0))],
        out_specs=pl.BlockSpec((tr, c), lambda i: (i, 0)),
        out_shape=jax.ShapeDtypeStruct((r, c), BF16),
        compiler_params=_params(("parallel",), 32),
        name="cast_bf16",
    )(w)


def _ep_gelu(acc):
    return 0.5 * acc * (1.0 + lax.erf(acc * SQRT_HALF))


def _ep_identity(acc):
    return acc


def _ep_headnorm(acc, gain_ref):
    outs = []
    for h in range(acc.shape[1] // HEAD_DIM):
        a = acc[:, h * HEAD_DIM:(h + 1) * HEAD_DIM]
        ms = jnp.mean(a * a, axis=-1, keepdims=True)
        outs.append(a * lax.rsqrt(ms + EPS) * gain_ref[...])
    return jnp.concatenate(outs, axis=1)


def _ep_gate(acc, bias_ref):
    return jax.nn.sigmoid(acc + bias_ref[...])


def _proj_kernel(h_ref, w_ref, *rest, epilogue, n_extra):
    extra = rest[:n_extra]
    o_ref = rest[n_extra]
    wb_ref = rest[n_extra + 1]

    @pl.when(pl.program_id(1) == 0)
    def _():
        wb_ref[...] = w_ref[...].astype(BF16)

    tm = h_ref.shape[0]
    step = min(tm, 256)
    for r0 in range(0, tm, step):
        acc = jnp.dot(h_ref[r0:r0 + step, :], wb_ref[...], preferred_element_type=F32)
        o_ref[r0:r0 + step, :] = epilogue(acc, *extra).astype(o_ref.dtype)


def _in_proj(h, w, col0, ncols, epilogue, extras, extra_specs, out_dtype, tm, name, tn=512):
    t, k = h.shape
    j0 = col0 // tn
    return pl.pallas_call(
        functools.partial(_proj_kernel, epilogue=epilogue, n_extra=len(extras)),
        grid=(ncols // tn, t // tm),
        in_specs=[pl.BlockSpec((tm, k), lambda j, i: (i, 0)),
                  pl.BlockSpec((k, tn), lambda j, i: (0, j0 + j))] + extra_specs,
        out_specs=pl.BlockSpec((tm, tn), lambda j, i: (i, j)),
        out_shape=jax.ShapeDtypeStruct((t, ncols), out_dtype),
        scratch_shapes=[pltpu.VMEM((k, tn), BF16)],
        compiler_params=_params(("arbitrary", "arbitrary"), 48),
        name=name,
    )(h, w, *extras)


def _token_mixer_inputs(x, norm1, w_in, b_in_gate, q_gain, k_gain, tm, tag):
    h = _rmsnorm_bf16(x, norm1, min(tm, 512))
    gain_spec = [pl.BlockSpec((1, HEAD_DIM), lambda j, i: (0, 0))]
    uv = _in_proj(h, w_in, COL_UV, 2 * A_WIDTH, _ep_gelu, [], [], BF16, tm, "proj_uv_" + tag)
    q = _in_proj(h, w_in, COL_Q, B_QKV, _ep_headnorm, [q_gain.reshape(1, HEAD_DIM)], gain_spec, F32, tm,
                 "proj_q_" + tag)
    k = _in_proj(h, w_in, COL_K, B_QKV, _ep_headnorm, [k_gain.reshape(1, HEAD_DIM)], gain_spec, F32, tm,
                 "proj_k_" + tag)
    v = _in_proj(h, w_in, COL_V, B_QKV, _ep_identity, [], [], F32, tm, "proj_v_" + tag)
    gates = _in_proj(h, w_in, COL_GATE, 2 * D_MODEL, _ep_gate, [b_in_gate.reshape(1, 2 * D_MODEL)],
                     [pl.BlockSpec((1, 512), lambda j, i: (0, j))], BF16, tm, "proj_gate_" + tag)
    return uv, q, k, v, gates


def _gmlp_kernel(uv_ref, lng_ref, lnb_ref, w_ref, bias_ref, a_ref, *vn_out, sample, rows):
    v = uv_ref[:, A_WIDTH:].astype(F32)
    xc = v - jnp.mean(v, axis=-1, keepdims=True)
    vn = xc * lax.rsqrt(jnp.mean(xc * xc, axis=-1, keepdims=True) + EPS) * lng_ref[...] + lnb_ref[...]
    if vn_out:
        vn_out[0][...] = vn
    vnb = vn.astype(BF16)
    row = lax.broadcasted_iota(I32, (CHUNK, CHUNK), 0)
    col = lax.broadcasted_iota(I32, (CHUNK, CHUNK), 1)
    if sample:
        mask = ((row >> 3) == (col >> 3)) & ((row & 7) >= (col & 7))
    else:
        mask = row >= col
    for g in range(A_GROUPS):
        wm = jnp.where(mask, w_ref[g], 0.0).astype(BF16)
        bcol = bias_ref[:, g:g + 1]
        cs = slice(g * CHUNK, (g + 1) * CHUNK)
        for c in range(rows // CHUNK):
            rs = slice(c * CHUNK, (c + 1) * CHUNK)
            s = jnp.dot(wm, vnb[rs, cs], preferred_element_type=F32) + bcol
            a_ref[rs, cs] = (uv_ref[rs, cs].astype(F32) * s).astype(a_ref.dtype)


def _gmlp(uv, ln_g, ln_b, w_mix, bias_t, sample, tag, rows=256):
    t = uv.shape[0]
    out_shape = [jax.ShapeDtypeStruct((t, A_WIDTH), BF16)]
    out_specs = [pl.BlockSpec((rows, A_WIDTH), lambda i: (i, 0))]
    if sample:
        out_shape.append(jax.ShapeDtypeStruct((t, A_WIDTH), F32))
        out_specs.append(pl.BlockSpec((rows, A_WIDTH), lambda i: (i, 0)))
    res = pl.pallas_call(
        functools.partial(_gmlp_kernel, sample=sample, rows=rows),
        grid=(t // rows,),
        in_specs=[pl.BlockSpec((rows, 2 * A_WIDTH), lambda i: (i, 0)),
                  pl.BlockSpec((1, A_WIDTH), lambda i: (0, 0)),
                  pl.BlockSpec((1, A_WIDTH), lambda i: (0, 0)),
                  pl.BlockSpec((A_GROUPS, CHUNK, CHUNK), lambda i: (0, 0, 0)),
                  pl.BlockSpec((CHUNK, A_GROUPS), lambda i: (0, 0))],
        out_specs=out_specs,
        out_shape=out_shape,
        compiler_params=_params(("parallel",), 32),
        name="gmlp_" + tag,
    )(uv, ln_g.reshape(1, A_WIDTH), ln_b.reshape(1, A_WIDTH), w_mix, bias_t)
    return res if sample else res[0]


def _attn_prompt_kernel(*refs, dil):
    nh = HEADS_PER_GROUP
    q_refs, k_refs, v_refs = refs[:nh], refs[nh:2 * nh], refs[2 * nh:3 * nh]
    o_ref, lse_ref, kprev, vprev, obuf = refs[3 * nh:]
    n = pl.program_id(1)
    L = CHUNK

    @pl.when(n == 0)
    def _():
        kprev[...] = jnp.zeros_like(kprev)
        vprev[...] = jnp.zeros_like(vprev)

    qi = lax.broadcasted_iota(I32, (L, 2 * L), 0)
    ki = lax.broadcasted_iota(I32, (L, 2 * L), 1)
    dist = qi + L - ki
    first_key = jnp.where(n > 0, 0, L)
    valid = (dist >= 0) & (dist <= L) & (ki >= first_key)
    lane = lax.broadcasted_iota(I32, (L, LANES), 1)

    def residue(r, carry):
        rows = pl.ds(r, L, stride=dil) if dil > 1 else pl.ds(0, L)
        lse_tile = jnp.zeros((L, LANES), F32)
        for h in range(nh):
            cs = slice(h * HEAD_DIM, (h + 1) * HEAD_DIM)
            qr = q_refs[h][0, rows, :].astype(BF16)
            kr = k_refs[h][0, rows, :].astype(BF16)
            vr = v_refs[h][0, rows, :].astype(BF16)
            kcat = jnp.concatenate([kprev[r, :, cs], kr], axis=0)
            vcat = jnp.concatenate([vprev[r, :, cs], vr], axis=0)
            kprev[r, :, cs] = kr
            vprev[r, :, cs] = vr
            s = lax.dot_general(qr, kcat, (((1,), (1,)), ((), ())), preferred_element_type=F32) * ATTN_SCALE
            s = jnp.where(valid, s, -jnp.inf)
            m = jnp.max(s, axis=-1, keepdims=True)
            p = jnp.exp(s - m)
            den = jnp.sum(p, axis=-1, keepdims=True)
            obuf[h, rows, :] = jnp.dot(p.astype(BF16), vcat, preferred_element_type=F32) / den
            lse_tile = jnp.where(lane == h, m + jnp.log(den), lse_tile)
        lse_ref[0, rows, :] = lse_tile
        return carry

    if dil == 1:
        residue(0, 0)
    else:
        lax.fori_loop(0, dil, residue, 0)
    for h in range(nh):
        o_ref[0, :, h * HEAD_DIM:(h + 1) * HEAD_DIM] = obuf[h]


def _attn_prompt(q, k, v, g, batch, seq):
    _, dil = B_PATTERNS[g]
    rows = CHUNK * dil
    nh = HEADS_PER_GROUP
    head_specs = [pl.BlockSpec((1, rows, HEAD_DIM), functools.partial(lambda b, n, c: (b, n, c), c=g * nh + h))
                  for h in range(nh)]
    out_map = lambda b, n: (b, n, 0)
    q3, k3, v3 = (t.reshape(batch, seq, B_QKV) for t in (q, k, v))
    o, lse = pl.pallas_call(
        functools.partial(_attn_prompt_kernel, dil=dil),
        grid=(batch, seq // rows),
        in_specs=head_specs * 3,
        out_specs=[pl.BlockSpec((1, rows, B_OUT), out_map), pl.BlockSpec((1, rows, LANES), out_map)],
        out_shape=[jax.ShapeDtypeStruct((batch, seq, B_OUT), F32),
                   jax.ShapeDtypeStruct((batch, seq, LANES), F32)],
        scratch_shapes=[pltpu.VMEM((dil, CHUNK, B_OUT), BF16), pltpu.VMEM((dil, CHUNK, B_OUT), BF16),
                        pltpu.VMEM((nh, rows, HEAD_DIM), F32)],
        compiler_params=_params(("parallel", "arbitrary"), 48),
        name="attn_prompt_g%d" % g,
    )(*([q3] * nh + [k3] * nh + [v3] * nh))
    return o.reshape(batch * seq, B_OUT), lse.reshape(batch * seq, LANES)


def _attn_sample_kernel(q_ref, k_ref, v_ref, c1_ref, c2_ref, c3_ref, b_ref):
    S = q_ref.shape[1]
    nh = HEADS_PER_GROUP
    for s in range(S):
        outs, lses = [], []
        for g, (win, dil) in enumerate(B_PATTERNS):
            hs = slice(g * nh, (g + 1) * nh)
            cref = (c1_ref, c2_ref, c3_ref)[g]
            m0 = s // dil
            new_rows = list(range(s % dil, s + 1, dil))
            kparts = [cref[0, 0, m0:, s % dil, 0]] + [k_ref[0, j:j + 1, hs, :] for j in new_rows]
            vparts = [cref[0, 0, m0:, s % dil, 1]] + [v_ref[0, j:j + 1, hs, :] for j in new_rows]
            kk = jnp.concatenate(kparts, axis=0)
            vv = jnp.concatenate(vparts, axis=0)
            sc = jnp.sum(kk * q_ref[0, s:s + 1, hs, :], axis=-1, keepdims=True) * ATTN_SCALE
            m = jnp.max(sc, axis=0, keepdims=True)
            p = jnp.exp(sc - m)
            den = jnp.sum(p, axis=0, keepdims=True)
            outs.append(jnp.sum(p * vv, axis=0, keepdims=True) / den)
            lses.append(m + jnp.log(den))
        mx = jnp.maximum(jnp.maximum(lses[0], lses[1]), lses[2])
        e = [jnp.exp(l - mx) for l in lses]
        den = e[0] + e[1] + e[2]
        b_ref[0, s:s + 1] = (e[0] / den) * outs[0] + (e[1] / den) * outs[1] + (e[2] / den) * outs[2]


def _attn_sample(q, k, v, caches, batch, seq):
    nh = HEADS_PER_GROUP
    assert seq <= CHUNK
    specs, views = [], []
    for c, (win, dil) in zip(caches, B_PATTERNS):
        assert c.shape[0] == 1 and c.shape[2] == win, "decode cache must hold exactly the pattern's window"
        nres = min(dil, seq)
        views.append(c.reshape(batch, 1, win // dil, dil, 2, nh, HEAD_DIM))
        specs.append(pl.BlockSpec((1, 1, win // dil, nres, 2, nh, HEAD_DIM), lambda b: (b, 0, 0, 0, 0, 0, 0)))
    qkv_spec = pl.BlockSpec((1, seq, B_HEADS, HEAD_DIM), lambda b: (b, 0, 0, 0))
    q4, k4, v4 = (t.reshape(batch, seq, B_HEADS, HEAD_DIM) for t in (q, k, v))
    return pl.pallas_call(
        _attn_sample_kernel,
        grid=(batch,),
        in_specs=[qkv_spec] * 3 + specs,
        out_specs=pl.BlockSpec((1, seq, nh, HEAD_DIM), lambda b: (b, 0, 0, 0)),
        out_shape=jax.ShapeDtypeStruct((batch, seq, nh, HEAD_DIM), F32),
        compiler_params=_params(("parallel",), 40),
        name="attn_sample",
    )(q4, k4, v4, *views)


def _combine_groups(o_refs, l_refs):
    comb = []
    for h in range(HEADS_PER_GROUP):
        cs = slice(h * HEAD_DIM, (h + 1) * HEAD_DIM)
        ls = [l[:, h:h + 1] for l in l_refs]
        mx = jnp.maximum(jnp.maximum(ls[0], ls[1]), ls[2])
        e = [jnp.exp(l - mx) for l in ls]
        den = e[0] + e[1] + e[2]
        comb.append(sum((e[g] / den) * o_refs[g][:, cs] for g in range(3)))
    return jnp.concatenate(comb, axis=1).astype(BF16)


def _mix_kernel(*refs, combine):
    if combine:
        a_ref = refs[0]
        b = _combine_groups(refs[1:4], refs[4:7])
        ga_ref, gb_ref, wa_ref, wb_ref, m_ref = refs[7:]
    else:
        a_ref, b_ref, ga_ref, gb_ref, wa_ref, wb_ref, m_ref = refs
        b = b_ref[...].astype(BF16)
    ap = jnp.dot(a_ref[...], wa_ref[...], preferred_element_type=F32)
    bp = jnp.dot(b, wb_ref[...], preferred_element_type=F32)
    m_ref[...] = (ga_ref[...].astype(F32) * ap + gb_ref[...].astype(F32) * bp).astype(m_ref.dtype)


def _mix(a, b_parts, gates, wa, wb, tm, tag):
    t = a.shape[0]
    combine = len(b_parts) > 1
    row = lambda w: pl.BlockSpec((tm, w), lambda i: (i, 0))
    if combine:
        b_specs = [row(B_OUT)] * 3 + [row(LANES)] * 3
    else:
        b_specs = [row(B_OUT)]
    return pl.pallas_call(
        functools.partial(_mix_kernel, combine=combine),
        grid=(t // tm,),
        in_specs=[row(A_WIDTH)] + b_specs + [
            pl.BlockSpec((tm, D_MODEL), lambda i: (i, 0)),
            pl.BlockSpec((tm, D_MODEL), lambda i: (i, 1)),
            pl.BlockSpec((A_WIDTH, D_MODEL), lambda i: (0, 0)),
            pl.BlockSpec((B_OUT, D_MODEL), lambda i: (0, 0))],
        out_specs=row(D_MODEL),
        out_shape=jax.ShapeDtypeStruct((t, D_MODEL), BF16),
        compiler_params=_params(("parallel",), 48),
        name="mix_" + tag,
    )(a, *b_parts, gates, gates, wa, wb)


def _out_kernel(xp_ref, xs_ref, mp_ref, ms_ref, wo_ref, g2_ref, wr_ref, rb_ref, x1_ref, h2_ref, lg_ref, *, n_prompt):
    is_prompt = pl.program_id(0) < n_prompt
    x = jnp.where(is_prompt, xp_ref[...], xs_ref[...])
    m = jnp.where(is_prompt, mp_ref[...], ms_ref[...])
    x1 = x + jnp.dot(m, wo_ref[...], preferred_element_type=F32)
    x1_ref[...] = x1
    h = x1 * lax.rsqrt(jnp.mean(x1 * x1, axis=-1, keepdims=True) + EPS) * g2_ref[...]
    h_hi = h.astype(BF16)
    h_lo = (h - h_hi.astype(F32)).astype(BF16)
    w = wr_ref[...]
    w_hi = w.astype(BF16)
    w_lo = (w - w_hi.astype(F32)).astype(BF16)
    lg = jnp.dot(h_hi, w_hi, preferred_element_type=F32)
    lg += jnp.dot(h_lo, w_hi, preferred_element_type=F32)
    lg += jnp.dot(h_hi, w_lo, preferred_element_type=F32)
    lg_ref[...] = lg + rb_ref[...]
    h2_ref[...] = h


def _out_proj(xp, xs, mp, ms, wo, norm2, router_w, router_b):
    tp, ts = xp.shape[0], xs.shape[0]
    tm = ts
    assert tp % tm == 0
    n_prompt = tp // tm
    t_all = tp + ts
    const = lambda r, c: pl.BlockSpec((r, c), lambda i: (0, 0))
    p_spec = pl.BlockSpec((tm, D_MODEL), lambda i: (jnp.minimum(i, n_prompt - 1), 0))
    s_spec = pl.BlockSpec((tm, D_MODEL), lambda i: (0, 0))
    return pl.pallas_call(
        functools.partial(_out_kernel, n_prompt=n_prompt),
        grid=(n_prompt + 1,),
        in_specs=[p_spec, s_spec, p_spec, s_spec,
                  const(D_MODEL, D_MODEL), const(1, D_MODEL), const(D_MODEL, N_EXPERTS), const(1, N_EXPERTS)],
        out_specs=[pl.BlockSpec((tm, D_MODEL), lambda i: (i, 0)),
                   pl.BlockSpec((tm, D_MODEL), lambda i: (i, 0)),
                   pl.BlockSpec((tm, N_EXPERTS), lambda i: (i, 0))],
        out_shape=[jax.ShapeDtypeStruct((t_all, D_MODEL), F32),
                   jax.ShapeDtypeStruct((t_all, D_MODEL), F32),
                   jax.ShapeDtypeStruct((t_all, N_EXPERTS), F32)],
        compiler_params=_params(("arbitrary",), 48),
        name="out_proj",
    )(xp, xs, mp, ms, wo, norm2.reshape(1, D_MODEL), router_w, router_b.reshape(1, N_EXPERTS))


def _route_kernel(lg_ref, idx_ref, gate_ref, cnt_ref, *, rows):
    @pl.when(pl.program_id(0) == 0)
    def _():
        cnt_ref[...] = jnp.zeros_like(cnt_ref)

    l = lg_ref[...]
    lane = lax.broadcasted_iota(I32, l.shape, 1)
    vals, idxs = [], []
    for _ in range(TOP_K):
        m = jnp.max(l, axis=-1, keepdims=True)
        idx = jnp.min(jnp.where(l == m, lane, N_EXPERTS), axis=-1, keepdims=True)
        vals.append(m)
        idxs.append(idx)
        l = jnp.where(lane == idx, -jnp.inf, l)
    es = [jnp.exp(v - vals[0]) for v in vals]
    den = es[0] + es[1] + es[2] + es[3]
    onehot = sum((lane == idx).astype(F32) for idx in idxs)
    r = lax.broadcasted_iota(I32, (rows, rows), 0)
    c = lax.broadcasted_iota(I32, (rows, rows), 1)
    tri = (r > c).astype(BF16)
    before = jnp.dot(tri, onehot.astype(BF16), preferred_element_type=F32) + cnt_ref[...]
    cnt_ref[...] += jnp.sum(onehot, axis=0, keepdims=True)
    wide = lax.broadcasted_iota(I32, (rows, LANES), 1)
    idx_tile = jnp.zeros((rows, LANES), I32)
    gate_tile = jnp.zeros((rows, LANES), F32)
    for k in range(TOP_K):
        rank = jnp.sum(jnp.where(lane == idxs[k], before, 0.0), axis=-1, keepdims=True).astype(I32)
        idx_tile = jnp.where(wide == k, idxs[k], idx_tile)
        idx_tile = jnp.where(wide == TOP_K + k, rank, idx_tile)
        gate_tile = jnp.where(wide == k, es[k] / den, gate_tile)
    idx_ref[...] = idx_tile
    gate_ref[...] = gate_tile


def _route(logits, rows=256):
    t = logits.shape[0]
    return pl.pallas_call(
        functools.partial(_route_kernel, rows=rows),
        grid=(t // rows,),
        in_specs=[pl.BlockSpec((rows, N_EXPERTS), lambda i: (i, 0))],
        out_specs=[pl.BlockSpec((rows, LANES), lambda i: (i, 0)),
                   pl.BlockSpec((rows, LANES), lambda i: (i, 0)),
                   pl.BlockSpec((1, N_EXPERTS), lambda i: (0, 0))],
        out_shape=[jax.ShapeDtypeStruct((t, LANES), I32),
                   jax.ShapeDtypeStruct((t, LANES), F32),
                   jax.ShapeDtypeStruct((1, N_EXPERTS), F32)],
        compiler_params=_params(("arbitrary",), 32),
        name="route",
    )(logits)


def _dispatch_kernel(dest_ref, h_ref, xs_hbm, sem, *, tm):
    def body(t, carry):
        src = h_ref.at[pl.ds(t, 1)]
        for k in range(TOP_K):
            pltpu.make_async_copy(src, xs_hbm.at[pl.ds(dest_ref[0, 0, t * TOP_K + k], 1)], sem).start()
        return carry

    lax.fori_loop(0, tm, body, 0)
    for k in range(TOP_K):
        pltpu.make_async_copy(h_ref, xs_hbm.at[pl.ds(0, tm)], sem).wait()


def _dispatch(h2, dest, n_slots, tm=1280):
    t, w = h2.shape
    assert t % tm == 0
    return pl.pallas_call(
        functools.partial(_dispatch_kernel, tm=tm),
        grid=(t // tm,),
        in_specs=[pl.BlockSpec((1, 1, tm * TOP_K), lambda i: (i, 0, 0), memory_space=pltpu.SMEM),
                  pl.BlockSpec((tm, w), lambda i: (i, 0))],
        out_specs=pl.BlockSpec(memory_space=pl.ANY),
        out_shape=jax.ShapeDtypeStruct((n_slots, w), h2.dtype),
        scratch_shapes=[pltpu.SemaphoreType.DMA],
        compiler_params=_params(("arbitrary",), 32),
        name="dispatch",
    )(dest.reshape(t // tm, 1, tm * TOP_K), h2)


def _moe_kernel(be_ref, bx_ref, nv_ref, x_ref, wg_ref, wu_ref, wd_ref, bg_ref, bu_ref, bd_ref, o_ref, xb_ref):
    del be_ref, bx_ref
    f = pl.program_id(1)
    nv = nv_ref[pl.program_id(0)]

    @pl.when((f == 0) & (nv > 0))
    def _():
        o_ref[...] = jnp.broadcast_to(bd_ref[0], o_ref.shape)

    def ffn(nrows):
        @pl.when(f == 0)
        def _():
            xb_ref[pl.ds(0, nrows), :] = x_ref[pl.ds(0, nrows), :].astype(BF16)

        rows = pl.ds(0, nrows)
        xb = xb_ref[rows, :]
        g = jnp.dot(xb, wg_ref[0].astype(BF16), preferred_element_type=F32) + bg_ref[0]
        u = jnp.dot(xb, wu_ref[0].astype(BF16), preferred_element_type=F32) + bu_ref[0]
        g = jnp.minimum(g, SWIGLU_LIMIT)
        u = jnp.clip(u, -SWIGLU_LIMIT, SWIGLU_LIMIT)
        act = ((u + 1.0) * (g * jax.nn.sigmoid(SWIGLU_ALPHA * g))).astype(BF16)
        o_ref[rows, :] += jnp.dot(act, wd_ref[0].astype(BF16), preferred_element_type=F32)

    for j in range(1, MOE_SUPER // MOE_SUB + 1):
        @pl.when(nv == j)
        def _():
            ffn(j * MOE_SUB)


def _moe(xs, blk_e, blk_x, blk_nv, w_gate, b_gate, w_up, b_up, w_down, b_down):
    n_super = xs.shape[0] // MOE_SUPER
    n_f = D_FF // MOE_TF
    last_f = n_f - 1

    def f_of(m, f, nv):
        return jnp.where(nv[m] > 0, f, last_f)

    grid_spec = pltpu.PrefetchScalarGridSpec(
        num_scalar_prefetch=3,
        grid=(n_super, n_f),
        in_specs=[
            pl.BlockSpec((MOE_SUPER, D_MODEL), lambda m, f, be, bx, nv: (bx[m], 0)),
            pl.BlockSpec((1, D_MODEL, MOE_TF), lambda m, f, be, bx, nv: (be[m], 0, f_of(m, f, nv))),
            pl.BlockSpec((1, D_MODEL, MOE_TF), lambda m, f, be, bx, nv: (be[m], 0, f_of(m, f, nv))),
            pl.BlockSpec((1, MOE_TF, D_MODEL), lambda m, f, be, bx, nv: (be[m], f_of(m, f, nv), 0)),
            pl.BlockSpec((1, 1, MOE_TF), lambda m, f, be, bx, nv: (be[m], 0, f_of(m, f, nv))),
            pl.BlockSpec((1, 1, MOE_TF), lambda m, f, be, bx, nv: (be[m], 0, f_of(m, f, nv))),
            pl.BlockSpec((1, 1, D_MODEL), lambda m, f, be, bx, nv: (be[m], 0, 0)),
        ],
        out_specs=pl.BlockSpec((MOE_SUPER, D_MODEL), lambda m, f, be, bx, nv: (bx[m], 0)),
        scratch_shapes=[pltpu.VMEM((MOE_SUPER, D_MODEL), BF16)],
    )
    return pl.pallas_call(
        _moe_kernel,
        grid_spec=grid_spec,
        out_shape=jax.ShapeDtypeStruct(xs.shape, F32),
        compiler_params=_params(("arbitrary", "arbitrary"), 60),
        name="moe_ffn",
    )(blk_e, blk_x, blk_nv, xs, w_gate, w_up, w_down,
      b_gate.reshape(N_EXPERTS, 1, D_FF), b_up.reshape(N_EXPERTS, 1, D_FF), b_down.reshape(N_EXPERTS, 1, D_MODEL))


def _combine_kernel(dest_ref, gate_ref, x1_ref, y_hbm, o_ref, buf, sem, *, tm):
    def body(t, carry):
        for k in range(TOP_K):
            pltpu.make_async_copy(y_hbm.at[pl.ds(dest_ref[0, 0, t * TOP_K + k], 1)],
                                  buf.at[k, pl.ds(t, 1)], sem).start()
        return carry

    lax.fori_loop(0, tm, body, 0)
    for k in range(TOP_K):
        pltpu.make_async_copy(y_hbm.at[pl.ds(0, tm)], buf.at[k], sem).wait()
    acc = x1_ref[...]
    for k in range(TOP_K):
        acc = acc + gate_ref[:, k:k + 1] * buf[k]
    o_ref[...] = acc


def _combine(dest, gates, x1, y, row0, t, tm, tag):
    t_all = x1.shape[0]
    blk0 = row0 // tm
    return pl.pallas_call(
        functools.partial(_combine_kernel, tm=tm),
        grid=(t // tm,),
        in_specs=[pl.BlockSpec((1, 1, tm * TOP_K), lambda i: (blk0 + i, 0, 0), memory_space=pltpu.SMEM),
                  pl.BlockSpec((tm, LANES), lambda i: (blk0 + i, 0)),
                  pl.BlockSpec((tm, D_MODEL), lambda i: (blk0 + i, 0)),
                  pl.BlockSpec(memory_space=pl.ANY)],
        out_specs=pl.BlockSpec((tm, D_MODEL), lambda i: (i, 0)),
        out_shape=jax.ShapeDtypeStruct((t, D_MODEL), F32),
        scratch_shapes=[pltpu.VMEM((TOP_K, tm, D_MODEL), F32), pltpu.SemaphoreType.DMA],
        compiler_params=_params(("arbitrary",), 40),
        name="combine_" + tag,
    )(dest.reshape(t_all // tm, 1, tm * TOP_K), gates, x1, y)


def _kv_state(k, v, g, batch, seq, keep):
    cs = slice(g * B_OUT, (g + 1) * B_OUT)
    k4 = k.reshape(batch, seq, B_QKV)[:, seq - keep:, cs].reshape(batch, keep, HEADS_PER_GROUP, HEAD_DIM)
    v4 = v.reshape(batch, seq, B_QKV)[:, seq - keep:, cs].reshape(batch, keep, HEADS_PER_GROUP, HEAD_DIM)
    return jnp.stack([k4, v4], axis=2)[None]


def kernel(x_prompt, x_sample, cache_kv_w128, cache_kv_w512, cache_kv_w2048, norm1, w_in, b_in_gate, gmlp_ln_g,
           gmlp_ln_b, gmlp_w_s, gmlp_b_s, q_gain, k_gain, w_a_out, w_b_out, w_o, norm2, router_w, router_b,
           exp_w_gate, exp_b_gate, exp_w_up, exp_b_up, exp_w_down, exp_b_down):
    assert norm1.shape[0] == 1, "single trunk layer"
    bp, sp, _ = x_prompt.shape
    bs, ss, _ = x_sample.shape
    tp, ts = bp * sp, bs * ss
    t_all = tp + ts
    xp = x_prompt.reshape(tp, D_MODEL)
    xs_ = x_sample.reshape(ts, D_MODEL)
    w_in2 = w_in[0]

    wa = _to_bf16(w_a_out[0])
    wb = _to_bf16(w_b_out[0])
    wo = _to_bf16(w_o[0])

    uv_p, q_p, k_p, v_p, gates_p = _token_mixer_inputs(xp, norm1[0], w_in2, b_in_gate[0], q_gain[0], k_gain[0],
                                                       1024, "p")
    a_p = _gmlp(uv_p, gmlp_ln_g[0], gmlp_ln_b[0], gmlp_w_s[0], gmlp_b_s[0].T, False, "p")
    o_parts, l_parts = [], []
    for g in range(len(B_PATTERNS)):
        o, lse = _attn_prompt(q_p, k_p, v_p, g, bp, sp)
        o_parts.append(o)
        l_parts.append(lse)
    m_p = _mix(a_p, o_parts + l_parts, gates_p, wa, wb, 512, "p")

    uv_s, q_s, k_s, v_s, gates_s = _token_mixer_inputs(xs_, norm1[0], w_in2, b_in_gate[0], q_gain[0], k_gain[0],
                                                       ts, "s")
    rep = CHUNK // ss
    w_mix_s = jnp.tile(gmlp_w_s[0][:, :ss, :ss], (1, rep, rep))
    bias_s = jnp.tile(gmlp_b_s[0][:, :ss].T, (rep, 1))
    a_s, vn_s = _gmlp(uv_s, gmlp_ln_g[0], gmlp_ln_b[0], w_mix_s, bias_s, True, "s")
    b_s = _attn_sample(q_s, k_s, v_s, (cache_kv_w128, cache_kv_w512, cache_kv_w2048), bs, ss)
    m_s = _mix(a_s, [b_s.reshape(ts, B_OUT)], gates_s, wa, wb, ts, "s")

    x1, h2, logits = _out_proj(xp, xs_, m_p, m_s, wo, norm2[0], router_w[0], router_b[0])
    y_p, y_s = _moe_layer(x1, h2, logits, tp, ts, exp_w_gate[0], exp_b_gate[0], exp_w_up[0], exp_b_up[0],
                          exp_w_down[0], exp_b_down[0])

    keep = [min(w, sp) for w, _ in B_PATTERNS]
    return (y_p.reshape(bp, sp, D_MODEL),
            y_s.reshape(bs, ss, D_MODEL),
            _kv_state(k_p, v_p, 0, bp, sp, keep[0]),
            _kv_state(k_p, v_p, 1, bp, sp, keep[1]),
            _kv_state(k_p, v_p, 2, bp, sp, keep[2]),
            _kv_state(k_s, v_s, 0, bs, ss, ss),
            _kv_state(k_s, v_s, 1, bs, ss, ss),
            _kv_state(k_s, v_s, 2, bs, ss, ss),
            vn_s.reshape(1, bs, ss, A_WIDTH))


def _moe_layer(x1, h2, logits, tp, ts, w_gate, b_gate, w_up, b_up, w_down, b_down):
    t_all = tp + ts
    idx_tile, gate_tile, counts = _route(logits)
    top_e = idx_tile[:, :TOP_K]
    rank = idx_tile[:, TOP_K:2 * TOP_K]
    counts = counts[0].astype(I32)
    sub_per_blk = MOE_SUPER // MOE_SUB
    nsub = (counts + MOE_SUB - 1) // MOE_SUB
    nblk = (nsub + sub_per_blk - 1) // sub_per_blk
    base = nsub // jnp.maximum(nblk, 1)
    extra = nsub - base * nblk
    blk_end = jnp.cumsum(nblk)
    blk_start = blk_end - nblk
    assert MOE_SUB & (MOE_SUB - 1) == 0
    sub_t = rank >> (MOE_SUB.bit_length() - 1)
    base_t, extra_t = jnp.maximum(base[top_e], 1), extra[top_e]
    in_big = sub_t < extra_t * (base_t + 1)
    num = jnp.where(in_big, sub_t, sub_t - extra_t * (base_t + 1))
    den = jnp.where(in_big, base_t + 1, base_t)
    quo = jnp.floor((num.astype(F32) + 0.5) / den.astype(F32)).astype(I32)
    blk_t = jnp.where(in_big, quo, extra_t + quo)
    off_t = num - quo * den
    dest = ((blk_start[top_e] + blk_t) * MOE_SUPER + off_t * MOE_SUB + (rank & (MOE_SUB - 1))).astype(I32)
    n_super = (t_all * TOP_K + N_EXPERTS * (MOE_SUPER - 1)) // MOE_SUPER
    sb = jnp.arange(n_super, dtype=I32)
    sb_valid = sb < blk_end[-1]
    sb_e = jnp.minimum(jnp.sum((blk_end[None, :] <= sb[:, None]).astype(I32), axis=1), N_EXPERTS - 1)
    sb_nv = jnp.where(sb_valid, base[sb_e] + ((sb - blk_start[sb_e]) < extra[sb_e]).astype(I32), 0).astype(I32)
    last = blk_end[-1] - 1
    sb_x = jnp.where(sb_valid, sb, last).astype(I32)
    sb_e = jnp.where(sb_valid, sb_e, sb_e[last]).astype(I32)

    xs_sorted = _dispatch(h2, dest, n_super * MOE_SUPER)
    y_sorted = _moe(xs_sorted, sb_e, sb_x, sb_nv, w_gate, b_gate, w_up, b_up, w_down, b_down)
    y_p = _combine(dest, gate_tile, x1, y_sorted, 0, tp, 256, "p")
    y_s = _combine(dest, gate_tile, x1, y_sorted, tp, ts, ts, "s")
    return y_p, y_s
```

```python
import functools

import jax
import jax.numpy as jnp
from jax import lax
from jax.experimental import pallas as pl
from jax.experimental.pallas import tpu as pltpu

F32 = jnp.float32
BF16 = jnp.bfloat16
I32 = jnp.int32

D_MODEL = 2048
CHUNK = 128
A_GROUPS = 8
A_WIDTH = 1024
HEAD_DIM = 128
B_PATTERNS = ((128, 1), (512, 4), (2048, 16))
HEADS_PER_GROUP = 4
B_HEADS = len(B_PATTERNS) * HEADS_PER_GROUP
B_QKV = B_HEADS * HEAD_DIM
B_OUT = HEADS_PER_GROUP * HEAD_DIM
ATTN_SCALE = HEAD_DIM ** -0.5
N_EXPERTS = 32
TOP_K = 4
D_FF = 2048
SWIGLU_LIMIT = 7.0
SWIGLU_ALPHA = 1.702
EPS = 1e-6
SQRT_HALF = 0.7071067811865476

COL_UV = 0
COL_Q = 2 * A_WIDTH
COL_K = COL_Q + B_QKV
COL_V = COL_K + B_QKV
COL_GATE = COL_V + B_QKV

LANES = 128
MOE_SUPER = 768
MOE_SUB = 256
MOE_TF = 512
MIB = 1 << 20


def _params(semantics, vmem_mib):
    return pltpu.CompilerParams(dimension_semantics=semantics, vmem_limit_bytes=vmem_mib * MIB)


def _rmsnorm_kernel(x_ref, g_ref, o_ref):
    x = x_ref[...]
    y = x * lax.rsqrt(jnp.mean(x * x, axis=-1, keepdims=True) + EPS)
    o_ref[...] = (y * g_ref[...]).astype(o_ref.dtype)


def _rmsnorm_bf16(x, g, tm):
    t, d = x.shape
    return pl.pallas_call(
        _rmsnorm_kernel,
        grid=(t // tm,),
        in_specs=[pl.BlockSpec((tm, d), lambda i: (i, 0)), pl.BlockSpec((1, d), lambda i: (0, 0))],
        out_specs=pl.BlockSpec((tm, d), lambda i: (i, 0)),
        out_shape=jax.ShapeDtypeStruct((t, d), BF16),
        compiler_params=_params(("parallel",), 32),
        name="rmsnorm1",
    )(x, g.reshape(1, d))


def _cast_kernel(w_ref, o_ref):
    o_ref[...] = w_ref[...].astype(o_ref.dtype)


def _to_bf16(w, tr=256):
    r, c = w.shape
    return pl.pallas_call(
        _cast_kernel,
        grid=(r // tr,),
        in_specs=[pl.BlockSpec((tr, c), lambda i: (i, 0))],
        out_specs=pl.BlockSpec((tr, c), lambda i: (i, 0)),
        out_shape=jax.ShapeDtypeStruct((r, c), BF16),
        compiler_params=_params(("parallel",), 32),
        name="cast_bf16",
    )(w)


def _ep_gelu(acc):
    return 0.5 * acc * (1.0 + lax.erf(acc * SQRT_HALF))


def _ep_identity(acc):
    return acc


def _ep_headnorm(acc, gain_ref):
    outs = []
    for h in range(acc.shape[1] // HEAD_DIM):
        a = acc[:, h * HEAD_DIM:(h + 1) * HEAD_DIM]
        ms = jnp.mean(a * a, axis=-1, keepdims=True)
        outs.append(a * lax.rsqrt(ms + EPS) * gain_ref[...])
    return jnp.concatenate(outs, axis=1)


def _ep_gate(acc, bias_ref):
    return jax.nn.sigmoid(acc + bias_ref[...])


def _proj_kernel(h_ref, w_ref, *rest, epilogue, n_extra):
    extra = rest[:n_extra]
    o_ref = rest[n_extra]
    wb_ref = rest[n_extra + 1]

    @pl.when(pl.program_id(1) == 0)
    def _():
        wb_ref[...] = w_ref[...].astype(BF16)

    tm = h_ref.shape[0]
    step = min(tm, 256)
    for r0 in range(0, tm, step):
        acc = jnp.dot(h_ref[r0:r0 + step, :], wb_ref[...], preferred_element_type=F32)
        o_ref[r0:r0 + step, :] = epilogue(acc, *extra).astype(o_ref.dtype)


def _in_proj(h, w, col0, ncols, epilogue, extras, extra_specs, out_dtype, tm, name, tn=512):
    t, k = h.shape
    j0 = col0 // tn
    return pl.pallas_call(
        functools.partial(_proj_kernel, epilogue=epilogue, n_extra=len(extras)),
        grid=(ncols // tn, t // tm),
        in_specs=[pl.BlockSpec((tm, k), lambda j, i: (i, 0)),
                  pl.BlockSpec((k, tn), lambda j, i: (0, j0 + j))] + extra_specs,
        out_specs=pl.BlockSpec((tm, tn), lambda j, i: (i, j)),
        out_shape=jax.ShapeDtypeStruct((t, ncols), out_dtype),
        scratch_shapes=[pltpu.VMEM((k, tn), BF16)],
        compiler_params=_params(("arbitrary", "arbitrary"), 48),
        name=name,
    )(h, w, *extras)


def _token_mixer_inputs(x, norm1, w_in, b_in_gate, q_gain, k_gain, tm, tag):
    h = _rmsnorm_bf16(x, norm1, min(tm, 512))
    gain_spec = [pl.BlockSpec((1, HEAD_DIM), lambda j, i: (0, 0))]
    uv = _in_proj(h, w_in, COL_UV, 2 * A_WIDTH, _ep_gelu, [], [], BF16, tm, "proj_uv_" + tag)
    q = _in_proj(h, w_in, COL_Q, B_QKV, _ep_headnorm, [q_gain.reshape(1, HEAD_DIM)], gain_spec, F32, tm,
                 "proj_q_" + tag)
    k = _in_proj(h, w_in, COL_K, B_QKV, _ep_headnorm, [k_gain.reshape(1, HEAD_DIM)], gain_spec, F32, tm,
                 "proj_k_" + tag)
    v = _in_proj(h, w_in, COL_V, B_QKV, _ep_identity, [], [], F32, tm, "proj_v_" + tag)
    gates = _in_proj(h, w_in, COL_GATE, 2 * D_MODEL, _ep_gate, [b_in_gate.reshape(1, 2 * D_MODEL)],
                     [pl.BlockSpec((1, 512), lambda j, i: (0, j))], BF16, tm, "proj_gate_" + tag)
    return uv, q, k, v, gates


def _gmlp_kernel(uv_ref, lng_ref, lnb_ref, w_ref, bias_ref, a_ref, *vn_out, sample, rows):
    v = uv_ref[:, A_WIDTH:].astype(F32)
    xc = v - jnp.mean(v, axis=-1, keepdims=True)
    vn = xc * lax.rsqrt(jnp.mean(xc * xc, axis=-1, keepdims=True) + EPS) * lng_ref[...] + lnb_ref[...]
    if vn_out:
        vn_out[0][...] = vn
    vnb = vn.astype(BF16)
    row = lax.broadcasted_iota(I32, (CHUNK, CHUNK), 0)
    col = lax.broadcasted_iota(I32, (CHUNK, CHUNK), 1)
    if sample:
        mask = ((row >> 3) == (col >> 3)) & ((row & 7) >= (col & 7))
    else:
        mask = row >= col
    for g in range(A_GROUPS):
        wm = jnp.where(mask, w_ref[g], 0.0).astype(BF16)
        bcol = bias_ref[:, g:g + 1]
        cs = slice(g * CHUNK, (g + 1) * CHUNK)
        for c in range(rows // CHUNK):
            rs = slice(c * CHUNK, (c + 1) * CHUNK)
            s = jnp.dot(wm, vnb[rs, cs], preferred_element_type=F32) + bcol
            a_ref[rs, cs] = (uv_ref[rs, cs].astype(F32) * s).astype(a_ref.dtype)


def _gmlp(uv, ln_g, ln_b, w_mix, bias_t, sample, tag, rows=256):
    t = uv.shape[0]
    out_shape = [jax.ShapeDtypeStruct((t, A_WIDTH), BF16)]
    out_specs = [pl.BlockSpec((rows, A_WIDTH), lambda i: (i, 0))]
    if sample:
        out_shape.append(jax.ShapeDtypeStruct((t, A_WIDTH), F32))
        out_specs.append(pl.BlockSpec((rows, A_WIDTH), lambda i: (i, 0)))
    res = pl.pallas_call(
        functools.partial(_gmlp_kernel, sample=sample, rows=rows),
        grid=(t // rows,),
        in_specs=[pl.BlockSpec((rows, 2 * A_WIDTH), lambda i: (i, 0)),
                  pl.BlockSpec((1, A_WIDTH), lambda i: (0, 0)),
                  pl.BlockSpec((1, A_WIDTH), lambda i: (0, 0)),
                  pl.BlockSpec((A_GROUPS, CHUNK, CHUNK), lambda i: (0, 0, 0)),
                  pl.BlockSpec((CHUNK, A_GROUPS), lambda i: (0, 0))],
        out_specs=out_specs,
        out_shape=out_shape,
        compiler_params=_params(("parallel",), 32),
        name="gmlp_" + tag,
    )(uv, ln_g.reshape(1, A_WIDTH), ln_b.reshape(1, A_WIDTH), w_mix, bias_t)
    return res if sample else res[0]


def _attn_prompt_kernel(*refs, dil, chunks):
    nh = HEADS_PER_GROUP
    q_refs, k_refs, v_refs = refs[:nh], refs[nh:2 * nh], refs[2 * nh:3 * nh]
    o_ref, lse_ref, kprev, vprev, obuf = refs[3 * nh:]
    n = pl.program_id(1)
    L = CHUNK

    @pl.when(n == 0)
    def _():
        kprev[...] = jnp.zeros_like(kprev)
        vprev[...] = jnp.zeros_like(vprev)

    qi = lax.broadcasted_iota(I32, (L, 2 * L), 0)
    ki = lax.broadcasted_iota(I32, (L, 2 * L), 1)
    dist = qi + L - ki
    in_window = (dist >= 0) & (dist <= L)
    first_key = jnp.where(n > 0, 0, L)
    in_window_first = in_window & (ki >= first_key)
    lane = lax.broadcasted_iota(I32, (L, LANES), 1)

    def one_chunk(rows, prev, valid):
        lse_tile = jnp.zeros((L, LANES), F32)
        own = []
        for h in range(nh):
            qr = q_refs[h][0, rows, :].astype(BF16)
            kr = k_refs[h][0, rows, :].astype(BF16)
            vr = v_refs[h][0, rows, :].astype(BF16)
            own.append((kr, vr))
            kcat = jnp.concatenate([prev[h][0], kr], axis=0)
            vcat = jnp.concatenate([prev[h][1], vr], axis=0)
            s = lax.dot_general(qr, kcat, (((1,), (1,)), ((), ())), preferred_element_type=F32) * ATTN_SCALE
            s = jnp.where(valid, s, -jnp.inf)
            m = jnp.max(s, axis=-1, keepdims=True)
            p = jnp.exp(s - m)
            den = jnp.sum(p, axis=-1, keepdims=True)
            obuf[h, rows, :] = jnp.dot(p.astype(BF16), vcat, preferred_element_type=F32) / den
            lse_tile = jnp.where(lane == h, m + jnp.log(den), lse_tile)
        lse_ref[0, rows, :] = lse_tile
        return own

    def load_prev(r):
        return [(kprev[r, :, h * HEAD_DIM:(h + 1) * HEAD_DIM], vprev[r, :, h * HEAD_DIM:(h + 1) * HEAD_DIM])
                for h in range(nh)]

    def store_prev(r, own):
        for h in range(nh):
            kprev[r, :, h * HEAD_DIM:(h + 1) * HEAD_DIM] = own[h][0]
            vprev[r, :, h * HEAD_DIM:(h + 1) * HEAD_DIM] = own[h][1]

    if dil == 1:
        prev = load_prev(0)
        for c in range(chunks):
            prev = one_chunk(pl.ds(c * L, L), prev, in_window_first if c == 0 else in_window)
        store_prev(0, prev)
    else:
        def residue(r, carry):
            store_prev(r, one_chunk(pl.ds(r, L, stride=dil), load_prev(r), in_window_first))
            return carry

        lax.fori_loop(0, dil, residue, 0, unroll=4)
    for h in range(nh):
        o_ref[0, :, h * HEAD_DIM:(h + 1) * HEAD_DIM] = obuf[h]


def _attn_prompt(q, k, v, g, batch, seq):
    _, dil = B_PATTERNS[g]
    chunks = 4 if dil == 1 else 1
    rows = CHUNK * dil * chunks
    nh = HEADS_PER_GROUP
    head_specs = [pl.BlockSpec((1, rows, HEAD_DIM), functools.partial(lambda b, n, c: (b, n, c), c=g * nh + h))
                  for h in range(nh)]
    out_map = lambda b, n: (b, n, 0)
    q3, k3, v3 = (t.reshape(batch, seq, B_QKV) for t in (q, k, v))
    o, lse = pl.pallas_call(
        functools.partial(_attn_prompt_kernel, dil=dil, chunks=chunks),
        grid=(batch, seq // rows),
        in_specs=head_specs * 3,
        out_specs=[pl.BlockSpec((1, rows, B_OUT), out_map), pl.BlockSpec((1, rows, LANES), out_map)],
        out_shape=[jax.ShapeDtypeStruct((batch, seq, B_OUT), F32),
                   jax.ShapeDtypeStruct((batch, seq, LANES), F32)],
        scratch_shapes=[pltpu.VMEM((dil, CHUNK, B_OUT), BF16), pltpu.VMEM((dil, CHUNK, B_OUT), BF16),
                        pltpu.VMEM((nh, rows, HEAD_DIM), F32)],
        compiler_params=_params(("parallel", "arbitrary"), 48),
        name="attn_prompt_g%d" % g,
    )(*([q3] * nh + [k3] * nh + [v3] * nh))
    return o.reshape(batch * seq, B_OUT), lse.reshape(batch * seq, LANES)


def _attn_sample_kernel(q_ref, k_ref, v_ref, c1_ref, c2_ref, c3_ref, b_ref):
    S = q_ref.shape[1]
    nh = HEADS_PER_GROUP
    for s in range(S):
        outs, lses = [], []
        for g, (win, dil) in enumerate(B_PATTERNS):
            hs = slice(g * nh, (g + 1) * nh)
            cref = (c1_ref, c2_ref, c3_ref)[g]
            m0 = s // dil
            new_rows = list(range(s % dil, s + 1, dil))
            kparts = [cref[0, 0, m0:, s % dil, 0]] + [k_ref[0, j:j + 1, hs, :] for j in new_rows]
            vparts = [cref[0, 0, m0:, s % dil, 1]] + [v_ref[0, j:j + 1, hs, :] for j in new_rows]
            kk = jnp.concatenate(kparts, axis=0)
            vv = jnp.concatenate(vparts, axis=0)
            sc = jnp.sum(kk * q_ref[0, s:s + 1, hs, :], axis=-1, keepdims=True) * ATTN_SCALE
            m = jnp.max(sc, axis=0, keepdims=True)
            p = jnp.exp(sc - m)
            den = jnp.sum(p, axis=0, keepdims=True)
            outs.append(jnp.sum(p * vv, axis=0, keepdims=True) / den)
            lses.append(m + jnp.log(den))
        mx = jnp.maximum(jnp.maximum(lses[0], lses[1]), lses[2])
        e = [jnp.exp(l - mx) for l in lses]
        den = e[0] + e[1] + e[2]
        b_ref[0, s:s + 1] = (e[0] / den) * outs[0] + (e[1] / den) * outs[1] + (e[2] / den) * outs[2]


def _attn_sample(q, k, v, caches, batch, seq):
    nh = HEADS_PER_GROUP
    assert seq <= CHUNK
    specs, views = [], []
    for c, (win, dil) in zip(caches, B_PATTERNS):
        assert c.shape[0] == 1 and c.shape[2] == win, "decode cache must hold exactly the pattern's window"
        nres = min(dil, seq)
        views.append(c.reshape(batch, 1, win // dil, dil, 2, nh, HEAD_DIM))
        specs.append(pl.BlockSpec((1, 1, win // dil, nres, 2, nh, HEAD_DIM), lambda b: (b, 0, 0, 0, 0, 0, 0)))
    qkv_spec = pl.BlockSpec((1, seq, B_HEADS, HEAD_DIM), lambda b: (b, 0, 0, 0))
    q4, k4, v4 = (t.reshape(batch, seq, B_HEADS, HEAD_DIM) for t in (q, k, v))
    return pl.pallas_call(
        _attn_sample_kernel,
        grid=(batch,),
        in_specs=[qkv_spec] * 3 + specs,
        out_specs=pl.BlockSpec((1, seq, nh, HEAD_DIM), lambda b: (b, 0, 0, 0)),
        out_shape=jax.ShapeDtypeStruct((batch, seq, nh, HEAD_DIM), F32),
        compiler_params=_params(("parallel",), 40),
        name="attn_sample",
    )(q4, k4, v4, *views)


def _combine_groups(o_refs, l_refs):
    comb = []
    for h in range(HEADS_PER_GROUP):
        cs = slice(h * HEAD_DIM, (h + 1) * HEAD_DIM)
        ls = [l[:, h:h + 1] for l in l_refs]
        mx = jnp.maximum(jnp.maximum(ls[0], ls[1]), ls[2])
        e = [jnp.exp(l - mx) for l in ls]
        den = e[0] + e[1] + e[2]
        comb.append(sum((e[g] / den) * o_refs[g][:, cs] for g in range(3)))
    return jnp.concatenate(comb, axis=1).astype(BF16)


def _mix_kernel(*refs, combine):
    if combine:
        a_ref = refs[0]
        b = _combine_groups(refs[1:4], refs[4:7])
        ga_ref, gb_ref, wa_ref, wb_ref, m_ref = refs[7:]
    else:
        a_ref, b_ref, ga_ref, gb_ref, wa_ref, wb_ref, m_ref = refs
        b = b_ref[...].astype(BF16)
    ap = jnp.dot(a_ref[...], wa_ref[...], preferred_element_type=F32)
    bp = jnp.dot(b, wb_ref[...], preferred_element_type=F32)
    m_ref[...] = (ga_ref[...].astype(F32) * ap + gb_ref[...].astype(F32) * bp).astype(m_ref.dtype)


def _mix(a, b_parts, gates, wa, wb, tm, tag):
    t = a.shape[0]
    combine = len(b_parts) > 1
    row = lambda w: pl.BlockSpec((tm, w), lambda i: (i, 0))
    if combine:
        b_specs = [row(B_OUT)] * 3 + [row(LANES)] * 3
    else:
        b_specs = [row(B_OUT)]
    return pl.pallas_call(
        functools.partial(_mix_kernel, combine=combine),
        grid=(t // tm,),
        in_specs=[row(A_WIDTH)] + b_specs + [
            pl.BlockSpec((tm, D_MODEL), lambda i: (i, 0)),
            pl.BlockSpec((tm, D_MODEL), lambda i: (i, 1)),
            pl.BlockSpec((A_WIDTH, D_MODEL), lambda i: (0, 0)),
            pl.BlockSpec((B_OUT, D_MODEL), lambda i: (0, 0))],
        out_specs=row(D_MODEL),
        out_shape=jax.ShapeDtypeStruct((t, D_MODEL), BF16),
        compiler_params=_params(("parallel",), 48),
        name="mix_" + tag,
    )(a, *b_parts, gates, gates, wa, wb)


def _out_kernel(xp_ref, xs_ref, mp_ref, ms_ref, wo_ref, g2_ref, wr_ref, rb_ref, x1_ref, h2_ref, lg_ref, *, n_prompt):
    is_prompt = pl.program_id(0) < n_prompt
    x = jnp.where(is_prompt, xp_ref[...], xs_ref[...])
    m = jnp.where(is_prompt, mp_ref[...], ms_ref[...])
    x1 = x + jnp.dot(m, wo_ref[...], preferred_element_type=F32)
    x1_ref[...] = x1
    h = x1 * lax.rsqrt(jnp.mean(x1 * x1, axis=-1, keepdims=True) + EPS) * g2_ref[...]
    h_hi = h.astype(BF16)
    h_lo = (h - h_hi.astype(F32)).astype(BF16)
    w = wr_ref[...]
    w_hi = w.astype(BF16)
    w_lo = (w - w_hi.astype(F32)).astype(BF16)
    lg = jnp.dot(h_hi, w_hi, preferred_element_type=F32)
    lg += jnp.dot(h_lo, w_hi, preferred_element_type=F32)
    lg += jnp.dot(h_hi, w_lo, preferred_element_type=F32)
    lg_ref[...] = lg + rb_ref[...]
    h2_ref[...] = h


def _out_proj(xp, xs, mp, ms, wo, norm2, router_w, router_b):
    tp, ts = xp.shape[0], xs.shape[0]
    tm = ts
    assert tp % tm == 0
    n_prompt = tp // tm
    t_all = tp + ts
    const = lambda r, c: pl.BlockSpec((r, c), lambda i: (0, 0))
    p_spec = pl.BlockSpec((tm, D_MODEL), lambda i: (jnp.minimum(i, n_prompt - 1), 0))
    s_spec = pl.BlockSpec((tm, D_MODEL), lambda i: (0, 0))
    return pl.pallas_call(
        functools.partial(_out_kernel, n_prompt=n_prompt),
        grid=(n_prompt + 1,),
        in_specs=[p_spec, s_spec, p_spec, s_spec,
                  const(D_MODEL, D_MODEL), const(1, D_MODEL), const(D_MODEL, N_EXPERTS), const(1, N_EXPERTS)],
        out_specs=[pl.BlockSpec((tm, D_MODEL), lambda i: (i, 0)),
                   pl.BlockSpec((tm, D_MODEL), lambda i: (i, 0)),
                   pl.BlockSpec((tm, N_EXPERTS), lambda i: (i, 0))],
        out_shape=[jax.ShapeDtypeStruct((t_all, D_MODEL), F32),
                   jax.ShapeDtypeStruct((t_all, D_MODEL), F32),
                   jax.ShapeDtypeStruct((t_all, N_EXPERTS), F32)],
        compiler_params=_params(("arbitrary",), 48),
        name="out_proj",
    )(xp, xs, mp, ms, wo, norm2.reshape(1, D_MODEL), router_w, router_b.reshape(1, N_EXPERTS))


def _route_kernel(lg_ref, idx_ref, gate_ref, cnt_ref, *, rows):
    @pl.when(pl.program_id(0) == 0)
    def _():
        cnt_ref[...] = jnp.zeros_like(cnt_ref)

    l = lg_ref[...]
    lane = lax.broadcasted_iota(I32, l.shape, 1)
    vals, idxs = [], []
    for _ in range(TOP_K):
        m = jnp.max(l, axis=-1, keepdims=True)
        idx = jnp.min(jnp.where(l == m, lane, N_EXPERTS), axis=-1, keepdims=True)
        vals.append(m)
        idxs.append(idx)
        l = jnp.where(lane == idx, -jnp.inf, l)
    es = [jnp.exp(v - vals[0]) for v in vals]
    den = es[0] + es[1] + es[2] + es[3]
    onehot = sum((lane == idx).astype(F32) for idx in idxs)
    r = lax.broadcasted_iota(I32, (rows, rows), 0)
    c = lax.broadcasted_iota(I32, (rows, rows), 1)
    tri = (r > c).astype(BF16)
    before = jnp.dot(tri, onehot.astype(BF16), preferred_element_type=F32) + cnt_ref[...]
    cnt_ref[...] += jnp.sum(onehot, axis=0, keepdims=True)
    wide = lax.broadcasted_iota(I32, (rows, LANES), 1)
    idx_tile = jnp.zeros((rows, LANES), I32)
    gate_tile = jnp.zeros((rows, LANES), F32)
    for k in range(TOP_K):
        rank = jnp.sum(jnp.where(lane == idxs[k], before, 0.0), axis=-1, keepdims=True).astype(I32)
        idx_tile = jnp.where(wide == k, idxs[k], idx_tile)
        idx_tile = jnp.where(wide == TOP_K + k, rank, idx_tile)
        gate_tile = jnp.where(wide == k, es[k] / den, gate_tile)
    idx_ref[...] = idx_tile
    gate_ref[...] = gate_tile


def _route(logits, rows=256):
    t = logits.shape[0]
    return pl.pallas_call(
        functools.partial(_route_kernel, rows=rows),
        grid=(t // rows,),
        in_specs=[pl.BlockSpec((rows, N_EXPERTS), lambda i: (i, 0))],
        out_specs=[pl.BlockSpec((rows, LANES), lambda i: (i, 0)),
                   pl.BlockSpec((rows, LANES), lambda i: (i, 0)),
                   pl.BlockSpec((1, N_EXPERTS), lambda i: (0, 0))],
        out_shape=[jax.ShapeDtypeStruct((t, LANES), I32),
                   jax.ShapeDtypeStruct((t, LANES), F32),
                   jax.ShapeDtypeStruct((1, N_EXPERTS), F32)],
        compiler_params=_params(("arbitrary",), 32),
        name="route",
    )(logits)


def _slot_kernel(idx_ref, tab_ref, dest_ref):
    idx = idx_ref[...]
    rows = idx.shape[0]
    lane = lax.broadcasted_iota(I32, (rows, LANES), 1)
    e = jnp.where(lane < TOP_K, idx, 0)
    rank = pltpu.roll(idx, LANES - TOP_K, 1)
    base, extra, start = (jnp.take_along_axis(jnp.broadcast_to(tab_ref[r:r + 1, :], (rows, LANES)), e, axis=1,
                                              mode="promise_in_bounds") for r in range(3))
    base = jnp.maximum(base, 1.0)
    sub = (rank >> (MOE_SUB.bit_length() - 1)).astype(F32)
    in_big = sub < extra * (base + 1.0)
    num = jnp.where(in_big, sub, sub - extra * (base + 1.0))
    den = jnp.where(in_big, base + 1.0, base)
    quo = jnp.floor((num + 0.5) / den)
    blk = jnp.where(in_big, quo, extra + quo)
    off = num - quo * den
    dest_ref[...] = ((start + blk) * MOE_SUPER + off * MOE_SUB).astype(I32) + (rank & (MOE_SUB - 1))


def _slots(idx_tile, tables, rows=256):
    t = idx_tile.shape[0]
    return pl.pallas_call(
        _slot_kernel,
        grid=(t // rows,),
        in_specs=[pl.BlockSpec((rows, LANES), lambda i: (i, 0)),
                  pl.BlockSpec(tables.shape, lambda i: (0, 0))],
        out_specs=pl.BlockSpec((rows, LANES), lambda i: (i, 0)),
        out_shape=jax.ShapeDtypeStruct((t, LANES), I32),
        compiler_params=_params(("parallel",), 32),
        name="slots",
    )(idx_tile, tables)


def _dispatch_kernel(dest_ref, h_ref, xs_hbm, sem, *, tm):
    def body(t, carry):
        src = h_ref.at[pl.ds(t, 1)]
        for k in range(TOP_K):
            pltpu.make_async_copy(src, xs_hbm.at[pl.ds(dest_ref[0, 0, t * TOP_K + k], 1)], sem).start()
        return carry

    lax.fori_loop(0, tm, body, 0)
    for k in range(TOP_K):
        pltpu.make_async_copy(h_ref, xs_hbm.at[pl.ds(0, tm)], sem).wait()


def _dispatch(h2, dest, n_slots, tm=1280):
    t, w = h2.shape
    assert t % tm == 0
    return pl.pallas_call(
        functools.partial(_dispatch_kernel, tm=tm),
        grid=(t // tm,),
        in_specs=[pl.BlockSpec((1, 1, tm * TOP_K), lambda i: (i, 0, 0), memory_space=pltpu.SMEM),
                  pl.BlockSpec((tm, w), lambda i: (i, 0))],
        out_specs=pl.BlockSpec(memory_space=pl.ANY),
        out_shape=jax.ShapeDtypeStruct((n_slots, w), h2.dtype),
        scratch_shapes=[pltpu.SemaphoreType.DMA],
        compiler_params=_params(("arbitrary",), 32),
        name="dispatch",
    )(dest.reshape(t // tm, 1, tm * TOP_K), h2)


def _moe_kernel(be_ref, bx_ref, nv_ref, x_ref, wg_ref, wu_ref, wd_ref, bg_ref, bu_ref, bd_ref, o_ref, xb_ref):
    del be_ref, bx_ref
    f = pl.program_id(1)
    nv = nv_ref[pl.program_id(0)]

    @pl.when((f == 0) & (nv > 0))
    def _():
        o_ref[...] = jnp.broadcast_to(bd_ref[0], o_ref.shape)

    def ffn(nrows):
        @pl.when(f == 0)
        def _():
            xb_ref[pl.ds(0, nrows), :] = x_ref[pl.ds(0, nrows), :].astype(BF16)

        rows = pl.ds(0, nrows)
        xb = xb_ref[rows, :]
        g = jnp.dot(xb, wg_ref[0].astype(BF16), preferred_element_type=F32) + bg_ref[0]
        u = jnp.dot(xb, wu_ref[0].astype(BF16), preferred_element_type=F32) + bu_ref[0]
        g = jnp.minimum(g, SWIGLU_LIMIT)
        u = jnp.clip(u, -SWIGLU_LIMIT, SWIGLU_LIMIT)
        act = ((u + 1.0) * (g * jax.nn.sigmoid(SWIGLU_ALPHA * g))).astype(BF16)
        o_ref[rows, :] += jnp.dot(act, wd_ref[0].astype(BF16), preferred_element_type=F32)

    for j in range(1, MOE_SUPER // MOE_SUB + 1):
        @pl.when(nv == j)
        def _():
            ffn(j * MOE_SUB)


def _moe(xs, blk_e, blk_x, blk_nv, w_gate, b_gate, w_up, b_up, w_down, b_down):
    n_super = xs.shape[0] // MOE_SUPER
    n_f = D_FF // MOE_TF
    last_f = n_f - 1

    def f_of(m, f, nv):
        return jnp.where(nv[m] > 0, f, last_f)

    grid_spec = pltpu.PrefetchScalarGridSpec(
        num_scalar_prefetch=3,
        grid=(n_super, n_f),
        in_specs=[
            pl.BlockSpec((MOE_SUPER, D_MODEL), lambda m, f, be, bx, nv: (bx[m], 0)),
            pl.BlockSpec((1, D_MODEL, MOE_TF), lambda m, f, be, bx, nv: (be[m], 0, f_of(m, f, nv))),
            pl.BlockSpec((1, D_MODEL, MOE_TF), lambda m, f, be, bx, nv: (be[m], 0, f_of(m, f, nv))),
            pl.BlockSpec((1, MOE_TF, D_MODEL), lambda m, f, be, bx, nv: (be[m], f_of(m, f, nv), 0)),
            pl.BlockSpec((1, 1, MOE_TF), lambda m, f, be, bx, nv: (be[m], 0, f_of(m, f, nv))),
            pl.BlockSpec((1, 1, MOE_TF), lambda m, f, be, bx, nv: (be[m], 0, f_of(m, f, nv))),
            pl.BlockSpec((1, 1, D_MODEL), lambda m, f, be, bx, nv: (be[m], 0, 0)),
        ],
        out_specs=pl.BlockSpec((MOE_SUPER, D_MODEL), lambda m, f, be, bx, nv: (bx[m], 0)),
        scratch_shapes=[pltpu.VMEM((MOE_SUPER, D_MODEL), BF16)],
    )
    return pl.pallas_call(
        _moe_kernel,
        grid_spec=grid_spec,
        out_shape=jax.ShapeDtypeStruct(xs.shape, F32),
        compiler_params=_params(("arbitrary", "arbitrary"), 60),
        name="moe_ffn",
    )(blk_e, blk_x, blk_nv, xs, w_gate, w_up, w_down,
      b_gate.reshape(N_EXPERTS, 1, D_FF), b_up.reshape(N_EXPERTS, 1, D_FF), b_down.reshape(N_EXPERTS, 1, D_MODEL))


def _combine_kernel(dest_ref, dest_next_ref, gate_ref, x1_ref, y_hbm, o_ref, buf, sems, *, tm, n_steps):
    i = pl.program_id(0)
    slot = i % 2

    def gather(dref, s):
        def body(t, carry):
            for k in range(TOP_K):
                pltpu.make_async_copy(y_hbm.at[pl.ds(dref[0, 0, t * TOP_K + k], 1)],
                                      buf.at[s, k, pl.ds(t, 1)], sems.at[s]).start()
            return carry

        lax.fori_loop(0, tm, body, 0)

    @pl.when(i == 0)
    def _():
        gather(dest_ref, 0)

    @pl.when(i + 1 < n_steps)
    def _():
        gather(dest_next_ref, 1 - slot)

    for k in range(TOP_K):
        pltpu.make_async_copy(y_hbm.at[pl.ds(0, tm)], buf.at[slot, k], sems.at[slot]).wait()
    acc = x1_ref[...]
    for k in range(TOP_K):
        acc = acc + gate_ref[:, k:k + 1] * buf[slot, k]
    o_ref[...] = acc


def _combine(dest, gates, x1, y, row0, t, tm, tag):
    t_all = x1.shape[0]
    blk0 = row0 // tm
    n_steps = t // tm
    dest3 = dest.reshape(t_all // tm, 1, tm * TOP_K)
    smem = lambda imap: pl.BlockSpec((1, 1, tm * TOP_K), imap, memory_space=pltpu.SMEM)
    return pl.pallas_call(
        functools.partial(_combine_kernel, tm=tm, n_steps=n_steps),
        grid=(n_steps,),
        in_specs=[smem(lambda i: (blk0 + i, 0, 0)),
                  smem(lambda i: (blk0 + jnp.minimum(i + 1, n_steps - 1), 0, 0)),
                  pl.BlockSpec((tm, LANES), lambda i: (blk0 + i, 0)),
                  pl.BlockSpec((tm, D_MODEL), lambda i: (blk0 + i, 0)),
                  pl.BlockSpec(memory_space=pl.ANY)],
        out_specs=pl.BlockSpec((tm, D_MODEL), lambda i: (i, 0)),
        out_shape=jax.ShapeDtypeStruct((t, D_MODEL), F32),
        scratch_shapes=[pltpu.VMEM((2, TOP_K, tm, D_MODEL), F32), pltpu.SemaphoreType.DMA((2,))],
        compiler_params=_params(("arbitrary",), 40),
        name="combine_" + tag,
    )(dest3, dest3, gates, x1, y)


def _kv_state(k, v, g, batch, seq, keep):
    cs = slice(g * B_OUT, (g + 1) * B_OUT)
    k4 = k.reshape(batch, seq, B_QKV)[:, seq - keep:, cs].reshape(batch, keep, HEADS_PER_GROUP, HEAD_DIM)
    v4 = v.reshape(batch, seq, B_QKV)[:, seq - keep:, cs].reshape(batch, keep, HEADS_PER_GROUP, HEAD_DIM)
    return jnp.stack([k4, v4], axis=2)[None]


def kernel(x_prompt, x_sample, cache_kv_w128, cache_kv_w512, cache_kv_w2048, norm1, w_in, b_in_gate, gmlp_ln_g,
           gmlp_ln_b, gmlp_w_s, gmlp_b_s, q_gain, k_gain, w_a_out, w_b_out, w_o, norm2, router_w, router_b,
           exp_w_gate, exp_b_gate, exp_w_up, exp_b_up, exp_w_down, exp_b_down):
    assert norm1.shape[0] == 1, "single trunk layer"
    bp, sp, _ = x_prompt.shape
    bs, ss, _ = x_sample.shape
    tp, ts = bp * sp, bs * ss
    t_all = tp + ts
    xp = x_prompt.reshape(tp, D_MODEL)
    xs_ = x_sample.reshape(ts, D_MODEL)
    w_in2 = w_in[0]

    wa = _to_bf16(w_a_out[0])
    wb = _to_bf16(w_b_out[0])
    wo = _to_bf16(w_o[0])

    uv_p, q_p, k_p, v_p, gates_p = _token_mixer_inputs(xp, norm1[0], w_in2, b_in_gate[0], q_gain[0], k_gain[0],
                                                       1024, "p")
    a_p = _gmlp(uv_p, gmlp_ln_g[0], gmlp_ln_b[0], gmlp_w_s[0], gmlp_b_s[0].T, False, "p")
    o_parts, l_parts = [], []
    for g in range(len(B_PATTERNS)):
        o, lse = _attn_prompt(q_p, k_p, v_p, g, bp, sp)
        o_parts.append(o)
        l_parts.append(lse)
    m_p = _mix(a_p, o_parts + l_parts, gates_p, wa, wb, 512, "p")

    uv_s, q_s, k_s, v_s, gates_s = _token_mixer_inputs(xs_, norm1[0], w_in2, b_in_gate[0], q_gain[0], k_gain[0],
                                                       ts, "s")
    rep = CHUNK // ss
    w_mix_s = jnp.tile(gmlp_w_s[0][:, :ss, :ss], (1, rep, rep))
    bias_s = jnp.tile(gmlp_b_s[0][:, :ss].T, (rep, 1))
    a_s, vn_s = _gmlp(uv_s, gmlp_ln_g[0], gmlp_ln_b[0], w_mix_s, bias_s, True, "s")
    b_s = _attn_sample(q_s, k_s, v_s, (cache_kv_w128, cache_kv_w512, cache_kv_w2048), bs, ss)
    m_s = _mix(a_s, [b_s.reshape(ts, B_OUT)], gates_s, wa, wb, ts, "s")

    x1, h2, logits = _out_proj(xp, xs_, m_p, m_s, wo, norm2[0], router_w[0], router_b[0])
    y_p, y_s = _moe_layer(x1, h2, logits, tp, ts, exp_w_gate[0], exp_b_gate[0], exp_w_up[0], exp_b_up[0],
                          exp_w_down[0], exp_b_down[0])

    keep = [min(w, sp) for w, _ in B_PATTERNS]
    return (y_p.reshape(bp, sp, D_MODEL),
            y_s.reshape(bs, ss, D_MODEL),
            _kv_state(k_p, v_p, 0, bp, sp, keep[0]),
            _kv_state(k_p, v_p, 1, bp, sp, keep[1]),
            _kv_state(k_p, v_p, 2, bp, sp, keep[2]),
            _kv_state(k_s, v_s, 0, bs, ss, ss),
            _kv_state(k_s, v_s, 1, bs, ss, ss),
            _kv_state(k_s, v_s, 2, bs, ss, ss),
            vn_s.reshape(1, bs, ss, A_WIDTH))


def _moe_layer(x1, h2, logits, tp, ts, w_gate, b_gate, w_up, b_up, w_down, b_down):
    t_all = tp + ts
    idx_tile, gate_tile, counts = _route(logits)
    assert MOE_SUB & (MOE_SUB - 1) == 0
    counts = counts[0].astype(I32)
    sub_per_blk = MOE_SUPER // MOE_SUB
    nsub = (counts + MOE_SUB - 1) // MOE_SUB
    nblk = (nsub + sub_per_blk - 1) // sub_per_blk
    base = nsub // jnp.maximum(nblk, 1)
    extra = nsub - base * nblk
    blk_end = jnp.cumsum(nblk)
    blk_start = blk_end - nblk
    n_super = (t_all * TOP_K + N_EXPERTS * (MOE_SUPER - 1)) // MOE_SUPER
    tables = jnp.zeros((8, LANES), F32).at[:3, :N_EXPERTS].set(jnp.stack([base, extra, blk_start]).astype(F32))
    dest = _slots(idx_tile, tables)[:, :TOP_K]
    sb = jnp.arange(n_super, dtype=I32)
    sb_valid = sb < blk_end[-1]
    sb_e = jnp.minimum(jnp.sum((blk_end[None, :] <= sb[:, None]).astype(I32), axis=1), N_EXPERTS - 1)
    sb_nv = jnp.where(sb_valid, base[sb_e] + ((sb - blk_start[sb_e]) < extra[sb_e]).astype(I32), 0).astype(I32)
    last = blk_end[-1] - 1
    sb_x = jnp.where(sb_valid, sb, last).astype(I32)
    sb_e = jnp.where(sb_valid, sb_e, sb_e[last]).astype(I32)

    xs_sorted = _dispatch(h2, dest, n_super * MOE_SUPER)
    y_sorted = _moe(xs_sorted, sb_e, sb_x, sb_nv, w_gate, b_gate, w_up, b_up, w_down, b_down)
    y_p = _combine(dest, gate_tile, x1, y_sorted, 0, tp, 256, "p")
    y_s = _combine(dest, gate_tile, x1, y_sorted, tp, ts, ts, "s")
    return y_p, y_s
```

```python
import functools

import jax
import jax.numpy as jnp
from jax import lax
from jax.experimental import pallas as pl
from jax.experimental.pallas import tpu as pltpu

F32 = jnp.float32
BF16 = jnp.bfloat16
I32 = jnp.int32

D_MODEL = 2048
CHUNK = 128
A_GROUPS = 8
A_WIDTH = 1024
HEAD_DIM = 128
B_PATTERNS = ((128, 1), (512, 4), (2048, 16))
HEADS_PER_GROUP = 4
B_HEADS = len(B_PATTERNS) * HEADS_PER_GROUP
B_QKV = B_HEADS * HEAD_DIM
B_OUT = HEADS_PER_GROUP * HEAD_DIM
ATTN_SCALE = HEAD_DIM ** -0.5
N_EXPERTS = 32
TOP_K = 4
D_FF = 2048
SWIGLU_LIMIT = 7.0
SWIGLU_ALPHA = 1.702
EPS = 1e-6
SQRT_HALF = 0.7071067811865476

COL_UV = 0
COL_Q = 2 * A_WIDTH
COL_K = COL_Q + B_QKV
COL_V = COL_K + B_QKV
COL_GATE = COL_V + B_QKV

LANES = 128
MOE_SUPER = 768
MOE_SUB = 256
MOE_TF = 512
MIB = 1 << 20


def _params(semantics, vmem_mib):
    return pltpu.CompilerParams(dimension_semantics=semantics, vmem_limit_bytes=vmem_mib * MIB)


def _rmsnorm_kernel(x_ref, g_ref, o_ref):
    x = x_ref[...]
    y = x * lax.rsqrt(jnp.mean(x * x, axis=-1, keepdims=True) + EPS)
    o_ref[...] = (y * g_ref[...]).astype(o_ref.dtype)


def _rmsnorm_bf16(x, g, tm):
    t, d = x.shape
    return pl.pallas_call(
        _rmsnorm_kernel,
        grid=(t // tm,),
        in_specs=[pl.BlockSpec((tm, d), lambda i: (i, 0)), pl.BlockSpec((1, d), lambda i: (0, 0))],
        out_specs=pl.BlockSpec((tm, d), lambda i: (i, 0)),
        out_shape=jax.ShapeDtypeStruct((t, d), BF16),
        compiler_params=_params(("parallel",), 32),
        name="rmsnorm1",
    )(x, g.reshape(1, d))


def _cast_kernel(w_ref, o_ref):
    o_ref[...] = w_ref[...].astype(o_ref.dtype)


def _to_bf16(w, tr=256):
    r, c = w.shape
    return pl.pallas_call(
        _cast_kernel,
        grid=(r // tr,),
        in_specs=[pl.BlockSpec((tr, c), lambda i: (i, 0))],
        out_specs=pl.BlockSpec((tr, c), lambda i: (i, 0)),
        out_shape=jax.ShapeDtypeStruct((r, c), BF16),
        compiler_params=_params(("parallel",), 32),
        name="cast_bf16",
    )(w)


def _ep_gelu(acc):
    return 0.5 * acc * (1.0 + lax.erf(acc * SQRT_HALF))


def _ep_identity(acc):
    return acc


def _ep_headnorm(acc, gain_ref):
    outs = []
    for h in range(acc.shape[1] // HEAD_DIM):
        a = acc[:, h * HEAD_DIM:(h + 1) * HEAD_DIM]
        ms = jnp.mean(a * a, axis=-1, keepdims=True)
        outs.append(a * lax.rsqrt(ms + EPS) * gain_ref[...])
    return jnp.concatenate(outs, axis=1)


def _ep_gate(acc, bias_ref):
    return jax.nn.sigmoid(acc + bias_ref[...])


def _proj_kernel(h_ref, w_ref, *rest, epilogue, n_extra):
    extra = rest[:n_extra]
    o_ref = rest[n_extra]
    wb_ref = rest[n_extra + 1]

    @pl.when(pl.program_id(1) == 0)
    def _():
        wb_ref[...] = w_ref[...].astype(BF16)

    tm = h_ref.shape[0]
    step = min(tm, 256)
    for r0 in range(0, tm, step):
        acc = jnp.dot(h_ref[r0:r0 + step, :], wb_ref[...], preferred_element_type=F32)
        o_ref[r0:r0 + step, :] = epilogue(acc, *extra).astype(o_ref.dtype)


def _in_proj(h, w, col0, ncols, epilogue, extras, extra_specs, out_dtype, tm, name, tn=512):
    t, k = h.shape
    j0 = col0 // tn
    return pl.pallas_call(
        functools.partial(_proj_kernel, epilogue=epilogue, n_extra=len(extras)),
        grid=(ncols // tn, t // tm),
        in_specs=[pl.BlockSpec((tm, k), lambda j, i: (i, 0)),
                  pl.BlockSpec((k, tn), lambda j, i: (0, j0 + j))] + extra_specs,
        out_specs=pl.BlockSpec((tm, tn), lambda j, i: (i, j)),
        out_shape=jax.ShapeDtypeStruct((t, ncols), out_dtype),
        scratch_shapes=[pltpu.VMEM((k, tn), BF16)],
        compiler_params=_params(("arbitrary", "arbitrary"), 48),
        name=name,
    )(h, w, *extras)


def _token_mixer_inputs(x, norm1, w_in, b_in_gate, q_gain, k_gain, tm, tag):
    h = _rmsnorm_bf16(x, norm1, min(tm, 512))
    gain_spec = [pl.BlockSpec((1, HEAD_DIM), lambda j, i: (0, 0))]
    uv = _in_proj(h, w_in, COL_UV, 2 * A_WIDTH, _ep_gelu, [], [], BF16, tm, "proj_uv_" + tag)
    q = _in_proj(h, w_in, COL_Q, B_QKV, _ep_headnorm, [q_gain.reshape(1, HEAD_DIM)], gain_spec, F32, tm,
                 "proj_q_" + tag)
    k = _in_proj(h, w_in, COL_K, B_QKV, _ep_headnorm, [k_gain.reshape(1, HEAD_DIM)], gain_spec, F32, tm,
                 "proj_k_" + tag)
    v = _in_proj(h, w_in, COL_V, B_QKV, _ep_identity, [], [], F32, tm, "proj_v_" + tag)
    gates = _in_proj(h, w_in, COL_GATE, 2 * D_MODEL, _ep_gate, [b_in_gate.reshape(1, 2 * D_MODEL)],
                     [pl.BlockSpec((1, 512), lambda j, i: (0, j))], BF16, tm, "proj_gate_" + tag)
    return uv, q, k, v, gates


def _gmlp_kernel(uv_ref, lng_ref, lnb_ref, w_ref, bias_ref, a_ref, *vn_out, sample, rows):
    v = uv_ref[:, A_WIDTH:].astype(F32)
    xc = v - jnp.mean(v, axis=-1, keepdims=True)
    vn = xc * lax.rsqrt(jnp.mean(xc * xc, axis=-1, keepdims=True) + EPS) * lng_ref[...] + lnb_ref[...]
    if vn_out:
        vn_out[0][...] = vn
    vnb = vn.astype(BF16)
    row = lax.broadcasted_iota(I32, (CHUNK, CHUNK), 0)
    col = lax.broadcasted_iota(I32, (CHUNK, CHUNK), 1)
    if sample:
        mask = ((row >> 3) == (col >> 3)) & ((row & 7) >= (col & 7))
    else:
        mask = row >= col
    for g in range(A_GROUPS):
        wm = jnp.where(mask, w_ref[g], 0.0).astype(BF16)
        bcol = bias_ref[:, g:g + 1]
        cs = slice(g * CHUNK, (g + 1) * CHUNK)
        for c in range(rows // CHUNK):
            rs = slice(c * CHUNK, (c + 1) * CHUNK)
            s = jnp.dot(wm, vnb[rs, cs], preferred_element_type=F32) + bcol
            a_ref[rs, cs] = (uv_ref[rs, cs].astype(F32) * s).astype(a_ref.dtype)


def _gmlp(uv, ln_g, ln_b, w_mix, bias_t, sample, tag, rows=256):
    t = uv.shape[0]
    out_shape = [jax.ShapeDtypeStruct((t, A_WIDTH), BF16)]
    out_specs = [pl.BlockSpec((rows, A_WIDTH), lambda i: (i, 0))]
    if sample:
        out_shape.append(jax.ShapeDtypeStruct((t, A_WIDTH), F32))
        out_specs.append(pl.BlockSpec((rows, A_WIDTH), lambda i: (i, 0)))
    res = pl.pallas_call(
        functools.partial(_gmlp_kernel, sample=sample, rows=rows),
        grid=(t // rows,),
        in_specs=[pl.BlockSpec((rows, 2 * A_WIDTH), lambda i: (i, 0)),
                  pl.BlockSpec((1, A_WIDTH), lambda i: (0, 0)),
                  pl.BlockSpec((1, A_WIDTH), lambda i: (0, 0)),
                  pl.BlockSpec((A_GROUPS, CHUNK, CHUNK), lambda i: (0, 0, 0)),
                  pl.BlockSpec((CHUNK, A_GROUPS), lambda i: (0, 0))],
        out_specs=out_specs,
        out_shape=out_shape,
        compiler_params=_params(("parallel",), 32),
        name="gmlp_" + tag,
    )(uv, ln_g.reshape(1, A_WIDTH), ln_b.reshape(1, A_WIDTH), w_mix, bias_t)
    return res if sample else res[0]


def _attn_prompt_kernel(*refs, dil, chunks):
    nh = HEADS_PER_GROUP
    q_refs, k_refs, v_refs = refs[:nh], refs[nh:2 * nh], refs[2 * nh:3 * nh]
    o_ref, lse_ref, kprev, vprev, obuf = refs[3 * nh:]
    n = pl.program_id(1)
    L = CHUNK

    @pl.when(n == 0)
    def _():
        kprev[...] = jnp.zeros_like(kprev)
        vprev[...] = jnp.zeros_like(vprev)

    qi = lax.broadcasted_iota(I32, (L, 2 * L), 0)
    ki = lax.broadcasted_iota(I32, (L, 2 * L), 1)
    dist = qi + L - ki
    in_window = (dist >= 0) & (dist <= L)
    first_key = jnp.where(n > 0, 0, L)
    in_window_first = in_window & (ki >= first_key)
    lane = lax.broadcasted_iota(I32, (L, LANES), 1)

    def one_chunk(rows, prev, valid):
        lse_tile = jnp.zeros((L, LANES), F32)
        own = []
        for h in range(nh):
            qr = q_refs[h][0, rows, :].astype(BF16)
            kr = k_refs[h][0, rows, :].astype(BF16)
            vr = v_refs[h][0, rows, :].astype(BF16)
            own.append((kr, vr))
            kcat = jnp.concatenate([prev[h][0], kr], axis=0)
            vcat = jnp.concatenate([prev[h][1], vr], axis=0)
            s = lax.dot_general(qr, kcat, (((1,), (1,)), ((), ())), preferred_element_type=F32) * ATTN_SCALE
            s = jnp.where(valid, s, -jnp.inf)
            m = jnp.max(s, axis=-1, keepdims=True)
            p = jnp.exp(s - m)
            den = jnp.sum(p, axis=-1, keepdims=True)
            obuf[h, rows, :] = jnp.dot(p.astype(BF16), vcat, preferred_element_type=F32) / den
            lse_tile = jnp.where(lane == h, m + jnp.log(den), lse_tile)
        lse_ref[0, rows, :] = lse_tile
        return own

    def load_prev(r):
        return [(kprev[r, :, h * HEAD_DIM:(h + 1) * HEAD_DIM], vprev[r, :, h * HEAD_DIM:(h + 1) * HEAD_DIM])
                for h in range(nh)]

    def store_prev(r, own):
        for h in range(nh):
            kprev[r, :, h * HEAD_DIM:(h + 1) * HEAD_DIM] = own[h][0]
            vprev[r, :, h * HEAD_DIM:(h + 1) * HEAD_DIM] = own[h][1]

    if dil == 1:
        prev = load_prev(0)
        for c in range(chunks):
            prev = one_chunk(pl.ds(c * L, L), prev, in_window_first if c == 0 else in_window)
        store_prev(0, prev)
    else:
        def residue(r, carry):
            store_prev(r, one_chunk(pl.ds(r, L, stride=dil), load_prev(r), in_window_first))
            return carry

        lax.fori_loop(0, dil, residue, 0, unroll=4)
    for h in range(nh):
        o_ref[0, :, h * HEAD_DIM:(h + 1) * HEAD_DIM] = obuf[h]


def _attn_prompt(q, k, v, g, batch, seq):
    _, dil = B_PATTERNS[g]
    chunks = 4 if dil == 1 else 1
    rows = CHUNK * dil * chunks
    nh = HEADS_PER_GROUP
    head_specs = [pl.BlockSpec((1, rows, HEAD_DIM), functools.partial(lambda b, n, c: (b, n, c), c=g * nh + h))
                  for h in range(nh)]
    out_map = lambda b, n: (b, n, 0)
    q3, k3, v3 = (t.reshape(batch, seq, B_QKV) for t in (q, k, v))
    o, lse = pl.pallas_call(
        functools.partial(_attn_prompt_kernel, dil=dil, chunks=chunks),
        grid=(batch, seq // rows),
        in_specs=head_specs * 3,
        out_specs=[pl.BlockSpec((1, rows, B_OUT), out_map), pl.BlockSpec((1, rows, LANES), out_map)],
        out_shape=[jax.ShapeDtypeStruct((batch, seq, B_OUT), F32),
                   jax.ShapeDtypeStruct((batch, seq, LANES), F32)],
        scratch_shapes=[pltpu.VMEM((dil, CHUNK, B_OUT), BF16), pltpu.VMEM((dil, CHUNK, B_OUT), BF16),
                        pltpu.VMEM((nh, rows, HEAD_DIM), F32)],
        compiler_params=_params(("parallel", "arbitrary"), 48),
        name="attn_prompt_g%d" % g,
    )(*([q3] * nh + [k3] * nh + [v3] * nh))
    return o.reshape(batch * seq, B_OUT), lse.reshape(batch * seq, LANES)


def _attn_sample_kernel(q_ref, k_ref, v_ref, c1_ref, c2_ref, c3_ref, b_ref):
    S = q_ref.shape[1]
    nh = HEADS_PER_GROUP
    for s in range(S):
        outs, lses = [], []
        for g, (win, dil) in enumerate(B_PATTERNS):
            hs = slice(g * nh, (g + 1) * nh)
            cref = (c1_ref, c2_ref, c3_ref)[g]
            m0 = s // dil
            new_rows = list(range(s % dil, s + 1, dil))
            kparts = [cref[0, 0, m0:, s % dil, 0]] + [k_ref[0, j:j + 1, hs, :] for j in new_rows]
            vparts = [cref[0, 0, m0:, s % dil, 1]] + [v_ref[0, j:j + 1, hs, :] for j in new_rows]
            kk = jnp.concatenate(kparts, axis=0)
            vv = jnp.concatenate(vparts, axis=0)
            sc = jnp.sum(kk * (q_ref[0, s:s + 1, hs, :] * ATTN_SCALE), axis=-1, keepdims=True)
            m = jnp.max(sc, axis=0, keepdims=True)
            p = jnp.exp(sc - m)
            den = jnp.sum(p, axis=0, keepdims=True)
            outs.append(jnp.sum(p * vv, axis=0, keepdims=True) / den)
            lses.append(m + jnp.log(den))
        mx = jnp.maximum(jnp.maximum(lses[0], lses[1]), lses[2])
        e = [jnp.exp(l - mx) for l in lses]
        den = e[0] + e[1] + e[2]
        b_ref[0, s:s + 1] = (e[0] / den) * outs[0] + (e[1] / den) * outs[1] + (e[2] / den) * outs[2]


def _attn_sample(q, k, v, caches, batch, seq):
    nh = HEADS_PER_GROUP
    assert seq <= CHUNK
    specs, views = [], []
    for c, (win, dil) in zip(caches, B_PATTERNS):
        assert c.shape[0] == 1 and c.shape[2] == win, "decode cache must hold exactly the pattern's window"
        nres = min(dil, seq)
        views.append(c.reshape(batch, 1, win // dil, dil, 2, nh, HEAD_DIM))
        specs.append(pl.BlockSpec((1, 1, win // dil, nres, 2, nh, HEAD_DIM), lambda b: (b, 0, 0, 0, 0, 0, 0)))
    qkv_spec = pl.BlockSpec((1, seq, B_HEADS, HEAD_DIM), lambda b: (b, 0, 0, 0))
    q4, k4, v4 = (t.reshape(batch, seq, B_HEADS, HEAD_DIM) for t in (q, k, v))
    return pl.pallas_call(
        _attn_sample_kernel,
        grid=(batch,),
        in_specs=[qkv_spec] * 3 + specs,
        out_specs=pl.BlockSpec((1, seq, nh, HEAD_DIM), lambda b: (b, 0, 0, 0)),
        out_shape=jax.ShapeDtypeStruct((batch, seq, nh, HEAD_DIM), F32),
        compiler_params=_params(("parallel",), 40),
        name="attn_sample",
    )(q4, k4, v4, *views)


def _combine_groups(o_refs, l_refs):
    comb = []
    for h in range(HEADS_PER_GROUP):
        cs = slice(h * HEAD_DIM, (h + 1) * HEAD_DIM)
        ls = [l[:, h:h + 1] for l in l_refs]
        mx = jnp.maximum(jnp.maximum(ls[0], ls[1]), ls[2])
        e = [jnp.exp(l - mx) for l in ls]
        den = e[0] + e[1] + e[2]
        comb.append(sum((e[g] / den) * o_refs[g][:, cs] for g in range(3)))
    return jnp.concatenate(comb, axis=1).astype(BF16)


def _mix_kernel(*refs, combine):
    if combine:
        a_ref = refs[0]
        b = _combine_groups(refs[1:4], refs[4:7])
        ga_ref, gb_ref, wa_ref, wb_ref, m_ref = refs[7:]
    else:
        a_ref, b_ref, ga_ref, gb_ref, wa_ref, wb_ref, m_ref = refs
        b = b_ref[...].astype(BF16)
    ap = jnp.dot(a_ref[...], wa_ref[...], preferred_element_type=F32)
    bp = jnp.dot(b, wb_ref[...], preferred_element_type=F32)
    m_ref[...] = (ga_ref[...].astype(F32) * ap + gb_ref[...].astype(F32) * bp).astype(m_ref.dtype)


def _mix(a, b_parts, gates, wa, wb, tm, tag):
    t = a.shape[0]
    combine = len(b_parts) > 1
    row = lambda w: pl.BlockSpec((tm, w), lambda i: (i, 0))
    if combine:
        b_specs = [row(B_OUT)] * 3 + [row(LANES)] * 3
    else:
        b_specs = [row(B_OUT)]
    return pl.pallas_call(
        functools.partial(_mix_kernel, combine=combine),
        grid=(t // tm,),
        in_specs=[row(A_WIDTH)] + b_specs + [
            pl.BlockSpec((tm, D_MODEL), lambda i: (i, 0)),
            pl.BlockSpec((tm, D_MODEL), lambda i: (i, 1)),
            pl.BlockSpec((A_WIDTH, D_MODEL), lambda i: (0, 0)),
            pl.BlockSpec((B_OUT, D_MODEL), lambda i: (0, 0))],
        out_specs=row(D_MODEL),
        out_shape=jax.ShapeDtypeStruct((t, D_MODEL), BF16),
        compiler_params=_params(("parallel",), 48),
        name="mix_" + tag,
    )(a, *b_parts, gates, gates, wa, wb)


def _out_kernel(xp_ref, xs_ref, mp_ref, ms_ref, wo_ref, g2_ref, wr_ref, rb_ref, x1_ref, h2_ref, lg_ref, *, n_prompt):
    is_prompt = pl.program_id(0) < n_prompt
    x = jnp.where(is_prompt, xp_ref[...], xs_ref[...])
    m = jnp.where(is_prompt, mp_ref[...], ms_ref[...])
    x1 = x + jnp.dot(m, wo_ref[...], preferred_element_type=F32)
    x1_ref[...] = x1
    h = x1 * lax.rsqrt(jnp.mean(x1 * x1, axis=-1, keepdims=True) + EPS) * g2_ref[...]
    h_hi = h.astype(BF16)
    h_lo = (h - h_hi.astype(F32)).astype(BF16)
    w = wr_ref[...]
    w_hi = w.astype(BF16)
    w_lo = (w - w_hi.astype(F32)).astype(BF16)
    lg = jnp.dot(h_hi, w_hi, preferred_element_type=F32)
    lg += jnp.dot(h_lo, w_hi, preferred_element_type=F32)
    lg += jnp.dot(h_hi, w_lo, preferred_element_type=F32)
    lg_ref[...] = lg + rb_ref[...]
    h2_ref[...] = h


def _out_proj(xp, xs, mp, ms, wo, norm2, router_w, router_b):
    tp, ts = xp.shape[0], xs.shape[0]
    tm = ts
    assert tp % tm == 0
    n_prompt = tp // tm
    t_all = tp + ts
    const = lambda r, c: pl.BlockSpec((r, c), lambda i: (0, 0))
    p_spec = pl.BlockSpec((tm, D_MODEL), lambda i: (jnp.minimum(i, n_prompt - 1), 0))
    s_spec = pl.BlockSpec((tm, D_MODEL), lambda i: (0, 0))
    return pl.pallas_call(
        functools.partial(_out_kernel, n_prompt=n_prompt),
        grid=(n_prompt + 1,),
        in_specs=[p_spec, s_spec, p_spec, s_spec,
                  const(D_MODEL, D_MODEL), const(1, D_MODEL), const(D_MODEL, N_EXPERTS), const(1, N_EXPERTS)],
        out_specs=[pl.BlockSpec((tm, D_MODEL), lambda i: (i, 0)),
                   pl.BlockSpec((tm, D_MODEL), lambda i: (i, 0)),
                   pl.BlockSpec((tm, N_EXPERTS), lambda i: (i, 0))],
        out_shape=[jax.ShapeDtypeStruct((t_all, D_MODEL), F32),
                   jax.ShapeDtypeStruct((t_all, D_MODEL), F32),
                   jax.ShapeDtypeStruct((t_all, N_EXPERTS), F32)],
        compiler_params=_params(("arbitrary",), 48),
        name="out_proj",
    )(xp, xs, mp, ms, wo, norm2.reshape(1, D_MODEL), router_w, router_b.reshape(1, N_EXPERTS))


def _route_kernel(lg_ref, idx_ref, gate_ref, cnt_ref, *, rows):
    @pl.when(pl.program_id(0) == 0)
    def _():
        cnt_ref[...] = jnp.zeros_like(cnt_ref)

    l = lg_ref[...]
    lane = lax.broadcasted_iota(I32, l.shape, 1)
    vals, idxs = [], []
    for _ in range(TOP_K):
        m = jnp.max(l, axis=-1, keepdims=True)
        idx = jnp.min(jnp.where(l == m, lane, N_EXPERTS), axis=-1, keepdims=True)
        vals.append(m)
        idxs.append(idx)
        l = jnp.where(lane == idx, -jnp.inf, l)
    es = [jnp.exp(v - vals[0]) for v in vals]
    den = es[0] + es[1] + es[2] + es[3]
    onehot = sum((lane == idx).astype(F32) for idx in idxs)
    r = lax.broadcasted_iota(I32, (rows, rows), 0)
    c = lax.broadcasted_iota(I32, (rows, rows), 1)
    tri = (r > c).astype(BF16)
    before = jnp.dot(tri, onehot.astype(BF16), preferred_element_type=F32) + cnt_ref[...]
    cnt_ref[...] += jnp.sum(onehot, axis=0, keepdims=True)
    wide = lax.broadcasted_iota(I32, (rows, LANES), 1)
    idx_tile = jnp.zeros((rows, LANES), I32)
    gate_tile = jnp.zeros((rows, LANES), F32)
    for k in range(TOP_K):
        rank = jnp.sum(jnp.where(lane == idxs[k], before, 0.0), axis=-1, keepdims=True).astype(I32)
        idx_tile = jnp.where(wide == k, idxs[k], idx_tile)
        idx_tile = jnp.where(wide == TOP_K + k, rank, idx_tile)
        gate_tile = jnp.where(wide == k, es[k] / den, gate_tile)
    idx_ref[...] = idx_tile
    gate_ref[...] = gate_tile


def _route(logits, rows=256):
    t = logits.shape[0]
    return pl.pallas_call(
        functools.partial(_route_kernel, rows=rows),
        grid=(t // rows,),
        in_specs=[pl.BlockSpec((rows, N_EXPERTS), lambda i: (i, 0))],
        out_specs=[pl.BlockSpec((rows, LANES), lambda i: (i, 0)),
                   pl.BlockSpec((rows, LANES), lambda i: (i, 0)),
                   pl.BlockSpec((1, N_EXPERTS), lambda i: (0, 0))],
        out_shape=[jax.ShapeDtypeStruct((t, LANES), I32),
                   jax.ShapeDtypeStruct((t, LANES), F32),
                   jax.ShapeDtypeStruct((1, N_EXPERTS), F32)],
        compiler_params=_params(("arbitrary",), 32),
        name="route",
    )(logits)


def _slot_kernel(idx_ref, tab_ref, dest_ref):
    idx = idx_ref[...]
    rows = idx.shape[0]
    lane = lax.broadcasted_iota(I32, (rows, LANES), 1)
    e = jnp.where(lane < TOP_K, idx, 0)
    rank = pltpu.roll(idx, LANES - TOP_K, 1)
    base, extra, start = (jnp.take_along_axis(jnp.broadcast_to(tab_ref[r:r + 1, :], (rows, LANES)), e, axis=1,
                                              mode="promise_in_bounds") for r in range(3))
    base = jnp.maximum(base, 1.0)
    sub = (rank >> (MOE_SUB.bit_length() - 1)).astype(F32)
    in_big = sub < extra * (base + 1.0)
    num = jnp.where(in_big, sub, sub - extra * (base + 1.0))
    den = jnp.where(in_big, base + 1.0, base)
    quo = jnp.floor((num + 0.5) / den)
    blk = jnp.where(in_big, quo, extra + quo)
    off = num - quo * den
    dest_ref[...] = ((start + blk) * MOE_SUPER + off * MOE_SUB).astype(I32) + (rank & (MOE_SUB - 1))


def _slots(idx_tile, tables, rows=256):
    t = idx_tile.shape[0]
    return pl.pallas_call(
        _slot_kernel,
        grid=(t // rows,),
        in_specs=[pl.BlockSpec((rows, LANES), lambda i: (i, 0)),
                  pl.BlockSpec(tables.shape, lambda i: (0, 0))],
        out_specs=pl.BlockSpec((rows, LANES), lambda i: (i, 0)),
        out_shape=jax.ShapeDtypeStruct((t, LANES), I32),
        compiler_params=_params(("parallel",), 32),
        name="slots",
    )(idx_tile, tables)


def _dispatch_kernel(dest_ref, h_ref, xs_hbm, sem, *, tm):
    def body(t, carry):
        src = h_ref.at[pl.ds(t, 1)]
        for k in range(TOP_K):
            pltpu.make_async_copy(src, xs_hbm.at[pl.ds(dest_ref[0, 0, t * TOP_K + k], 1)], sem).start()
        return carry

    lax.fori_loop(0, tm, body, 0)
    for k in range(TOP_K):
        pltpu.make_async_copy(h_ref, xs_hbm.at[pl.ds(0, tm)], sem).wait()


def _dispatch(h2, dest, n_slots, tm=1280):
    t, w = h2.shape
    assert t % tm == 0
    return pl.pallas_call(
        functools.partial(_dispatch_kernel, tm=tm),
        grid=(t // tm,),
        in_specs=[pl.BlockSpec((1, 1, tm * TOP_K), lambda i: (i, 0, 0), memory_space=pltpu.SMEM),
                  pl.BlockSpec((tm, w), lambda i: (i, 0))],
        out_specs=pl.BlockSpec(memory_space=pl.ANY),
        out_shape=jax.ShapeDtypeStruct((n_slots, w), h2.dtype),
        scratch_shapes=[pltpu.SemaphoreType.DMA],
        compiler_params=_params(("arbitrary",), 32),
        name="dispatch",
    )(dest.reshape(t // tm, 1, tm * TOP_K), h2)


def _moe_kernel(be_ref, bx_ref, nv_ref, x_ref, wg_ref, wu_ref, wd_ref, bg_ref, bu_ref, bd_ref, o_ref, xb_ref):
    del be_ref, bx_ref
    f = pl.program_id(1)
    nv = nv_ref[pl.program_id(0)]

    @pl.when((f == 0) & (nv > 0))
    def _():
        o_ref[...] = jnp.broadcast_to(bd_ref[0], o_ref.shape)

    def ffn(nrows):
        @pl.when(f == 0)
        def _():
            xb_ref[pl.ds(0, nrows), :] = x_ref[pl.ds(0, nrows), :].astype(BF16)

        rows = pl.ds(0, nrows)
        xb = xb_ref[rows, :]
        g = jnp.dot(xb, wg_ref[0].astype(BF16), preferred_element_type=F32) + bg_ref[0]
        u = jnp.dot(xb, wu_ref[0].astype(BF16), preferred_element_type=F32) + bu_ref[0]
        g = jnp.minimum(g, SWIGLU_LIMIT)
        u = jnp.clip(u, -SWIGLU_LIMIT, SWIGLU_LIMIT)
        act = ((u + 1.0) * (g * jax.nn.sigmoid(SWIGLU_ALPHA * g))).astype(BF16)
        o_ref[rows, :] += jnp.dot(act, wd_ref[0].astype(BF16), preferred_element_type=F32)

    for j in range(1, MOE_SUPER // MOE_SUB + 1):
        @pl.when(nv == j)
        def _():
            ffn(j * MOE_SUB)


def _moe(xs, blk_e, blk_x, blk_nv, w_gate, b_gate, w_up, b_up, w_down, b_down):
    n_super = xs.shape[0] // MOE_SUPER
    n_f = D_FF // MOE_TF
    last_f = n_f - 1

    def f_of(m, f, bx, nv):
        flip = (bx[m] & 1) == 1
        return jnp.where(nv[m] > 0, jnp.where(flip, last_f - f, f), jnp.where(flip, 0, last_f))

    grid_spec = pltpu.PrefetchScalarGridSpec(
        num_scalar_prefetch=3,
        grid=(n_super, n_f),
        in_specs=[
            pl.BlockSpec((MOE_SUPER, D_MODEL), lambda m, f, be, bx, nv: (bx[m], 0)),
            pl.BlockSpec((1, D_MODEL, MOE_TF), lambda m, f, be, bx, nv: (be[m], 0, f_of(m, f, bx, nv))),
            pl.BlockSpec((1, D_MODEL, MOE_TF), lambda m, f, be, bx, nv: (be[m], 0, f_of(m, f, bx, nv))),
            pl.BlockSpec((1, MOE_TF, D_MODEL), lambda m, f, be, bx, nv: (be[m], f_of(m, f, bx, nv), 0)),
            pl.BlockSpec((1, 1, MOE_TF), lambda m, f, be, bx, nv: (be[m], 0, f_of(m, f, bx, nv))),
            pl.BlockSpec((1, 1, MOE_TF), lambda m, f, be, bx, nv: (be[m], 0, f_of(m, f, bx, nv))),
            pl.BlockSpec((1, 1, D_MODEL), lambda m, f, be, bx, nv: (be[m], 0, 0)),
        ],
        out_specs=pl.BlockSpec((MOE_SUPER, D_MODEL), lambda m, f, be, bx, nv: (bx[m], 0)),
        scratch_shapes=[pltpu.VMEM((MOE_SUPER, D_MODEL), BF16)],
    )
    return pl.pallas_call(
        _moe_kernel,
        grid_spec=grid_spec,
        out_shape=jax.ShapeDtypeStruct(xs.shape, F32),
        compiler_params=_params(("arbitrary", "arbitrary"), 60),
        name="moe_ffn",
    )(blk_e, blk_x, blk_nv, xs, w_gate, w_up, w_down,
      b_gate.reshape(N_EXPERTS, 1, D_FF), b_up.reshape(N_EXPERTS, 1, D_FF), b_down.reshape(N_EXPERTS, 1, D_MODEL))


def _combine_kernel(dest_ref, dest_next_ref, gate_ref, x1_ref, y_hbm, o_ref, buf, sems, *, tm, n_steps):
    i = pl.program_id(0)
    slot = i % 2

    def gather(dref, s):
        def body(t, carry):
            for k in range(TOP_K):
                pltpu.make_async_copy(y_hbm.at[pl.ds(dref[0, 0, t * TOP_K + k], 1)],
                                      buf.at[s, k, pl.ds(t, 1)], sems.at[s]).start()
            return carry

        lax.fori_loop(0, tm, body, 0)

    @pl.when(i == 0)
    def _():
        gather(dest_ref, 0)

    @pl.when(i + 1 < n_steps)
    def _():
        gather(dest_next_ref, 1 - slot)

    for k in range(TOP_K):
        pltpu.make_async_copy(y_hbm.at[pl.ds(0, tm)], buf.at[slot, k], sems.at[slot]).wait()
    acc = x1_ref[...]
    for k in range(TOP_K):
        acc = acc + gate_ref[:, k:k + 1] * buf[slot, k]
    o_ref[...] = acc


def _combine(dest, gates, x1, y, row0, t, tm, tag):
    t_all = x1.shape[0]
    blk0 = row0 // tm
    n_steps = t // tm
    dest3 = dest.reshape(t_all // tm, 1, tm * TOP_K)
    smem = lambda imap: pl.BlockSpec((1, 1, tm * TOP_K), imap, memory_space=pltpu.SMEM)
    return pl.pallas_call(
        functools.partial(_combine_kernel, tm=tm, n_steps=n_steps),
        grid=(n_steps,),
        in_specs=[smem(lambda i: (blk0 + i, 0, 0)),
                  smem(lambda i: (blk0 + jnp.minimum(i + 1, n_steps - 1), 0, 0)),
                  pl.BlockSpec((tm, LANES), lambda i: (blk0 + i, 0)),
                  pl.BlockSpec((tm, D_MODEL), lambda i: (blk0 + i, 0)),
                  pl.BlockSpec(memory_space=pl.ANY)],
        out_specs=pl.BlockSpec((tm, D_MODEL), lambda i: (i, 0)),
        out_shape=jax.ShapeDtypeStruct((t, D_MODEL), F32),
        scratch_shapes=[pltpu.VMEM((2, TOP_K, tm, D_MODEL), F32), pltpu.SemaphoreType.DMA((2,))],
        compiler_params=_params(("arbitrary",), 40),
        name="combine_" + tag,
    )(dest3, dest3, gates, x1, y)


def _kv_state(k, v, g, batch, seq, keep):
    cs = slice(g * B_OUT, (g + 1) * B_OUT)
    k4 = k.reshape(batch, seq, B_QKV)[:, seq - keep:, cs].reshape(batch, keep, HEADS_PER_GROUP, HEAD_DIM)
    v4 = v.reshape(batch, seq, B_QKV)[:, seq - keep:, cs].reshape(batch, keep, HEADS_PER_GROUP, HEAD_DIM)
    return jnp.stack([k4, v4], axis=2)[None]


def kernel(x_prompt, x_sample, cache_kv_w128, cache_kv_w512, cache_kv_w2048, norm1, w_in, b_in_gate, gmlp_ln_g,
           gmlp_ln_b, gmlp_w_s, gmlp_b_s, q_gain, k_gain, w_a_out, w_b_out, w_o, norm2, router_w, router_b,
           exp_w_gate, exp_b_gate, exp_w_up, exp_b_up, exp_w_down, exp_b_down):
    assert norm1.shape[0] == 1, "single trunk layer"
    bp, sp, _ = x_prompt.shape
    bs, ss, _ = x_sample.shape
    tp, ts = bp * sp, bs * ss
    t_all = tp + ts
    xp = x_prompt.reshape(tp, D_MODEL)
    xs_ = x_sample.reshape(ts, D_MODEL)
    w_in2 = w_in[0]

    wa = _to_bf16(w_a_out[0])
    wb = _to_bf16(w_b_out[0])
    wo = _to_bf16(w_o[0])

    uv_p, q_p, k_p, v_p, gates_p = _token_mixer_inputs(xp, norm1[0], w_in2, b_in_gate[0], q_gain[0], k_gain[0],
                                                       2048, "p")
    a_p = _gmlp(uv_p, gmlp_ln_g[0], gmlp_ln_b[0], gmlp_w_s[0], gmlp_b_s[0].T, False, "p")
    o_parts, l_parts = [], []
    for g in range(len(B_PATTERNS)):
        o, lse = _attn_prompt(q_p, k_p, v_p, g, bp, sp)
        o_parts.append(o)
        l_parts.append(lse)
    m_p = _mix(a_p, o_parts + l_parts, gates_p, wa, wb, 512, "p")

    uv_s, q_s, k_s, v_s, gates_s = _token_mixer_inputs(xs_, norm1[0], w_in2, b_in_gate[0], q_gain[0], k_gain[0],
                                                       ts, "s")
    rep = CHUNK // ss
    w_mix_s = jnp.tile(gmlp_w_s[0][:, :ss, :ss], (1, rep, rep))
    bias_s = jnp.tile(gmlp_b_s[0][:, :ss].T, (rep, 1))
    a_s, vn_s = _gmlp(uv_s, gmlp_ln_g[0], gmlp_ln_b[0], w_mix_s, bias_s, True, "s")
    b_s = _attn_sample(q_s, k_s, v_s, (cache_kv_w128, cache_kv_w512, cache_kv_w2048), bs, ss)
    m_s = _mix(a_s, [b_s.reshape(ts, B_OUT)], gates_s, wa, wb, ts, "s")

    x1, h2, logits = _out_proj(xp, xs_, m_p, m_s, wo, norm2[0], router_w[0], router_b[0])
    y_p, y_s = _moe_layer(x1, h2, logits, tp, ts, exp_w_gate[0], exp_b_gate[0], exp_w_up[0], exp_b_up[0],
                          exp_w_down[0], exp_b_down[0])

    keep = [min(w, sp) for w, _ in B_PATTERNS]
    return (y_p.reshape(bp, sp, D_MODEL),
            y_s.reshape(bs, ss, D_MODEL),
            _kv_state(k_p, v_p, 0, bp, sp, keep[0]),
            _kv_state(k_p, v_p, 1, bp, sp, keep[1]),
            _kv_state(k_p, v_p, 2, bp, sp, keep[2]),
            _kv_state(k_s, v_s, 0, bs, ss, ss),
            _kv_state(k_s, v_s, 1, bs, ss, ss),
            _kv_state(k_s, v_s, 2, bs, ss, ss),
            vn_s.reshape(1, bs, ss, A_WIDTH))


def _moe_layer(x1, h2, logits, tp, ts, w_gate, b_gate, w_up, b_up, w_down, b_down):
    t_all = tp + ts
    idx_tile, gate_tile, counts = _route(logits)
    assert MOE_SUB & (MOE_SUB - 1) == 0
    counts = counts[0].astype(I32)
    sub_per_blk = MOE_SUPER // MOE_SUB
    nsub = (counts + MOE_SUB - 1) // MOE_SUB
    nblk = (nsub + sub_per_blk - 1) // sub_per_blk
    base = nsub // jnp.maximum(nblk, 1)
    extra = nsub - base * nblk
    blk_end = jnp.cumsum(nblk)
    blk_start = blk_end - nblk
    n_super = (t_all * TOP_K + N_EXPERTS * (MOE_SUPER - 1)) // MOE_SUPER
    tables = jnp.zeros((8, LANES), F32).at[:3, :N_EXPERTS].set(jnp.stack([base, extra, blk_start]).astype(F32))
    dest = _slots(idx_tile, tables)[:, :TOP_K]
    sb = jnp.arange(n_super, dtype=I32)
    sb_valid = sb < blk_end[-1]
    sb_e = jnp.minimum(jnp.sum((blk_end[None, :] <= sb[:, None]).astype(I32), axis=1), N_EXPERTS - 1)
    sb_nv = jnp.where(sb_valid, base[sb_e] + ((sb - blk_start[sb_e]) < extra[sb_e]).astype(I32), 0).astype(I32)
    last = blk_end[-1] - 1
    sb_x = jnp.where(sb_valid, sb, last).astype(I32)
    sb_e = jnp.where(sb_valid, sb_e, sb_e[last]).astype(I32)

    xs_sorted = _dispatch(h2, dest, n_super * MOE_SUPER)
    y_sorted = _moe(xs_sorted, sb_e, sb_x, sb_nv, w_gate, b_gate, w_up, b_up, w_down, b_down)
    y_p = _combine(dest, gate_tile, x1, y_sorted, 0, tp, 256, "p")
    y_s = _combine(dest, gate_tile, x1, y_sorted, tp, ts, ts, "s")
    return y_p, y_s
```

```python
import functools

import jax
import jax.numpy as jnp
from jax import lax
from jax.experimental import pallas as pl
from jax.experimental.pallas import tpu as pltpu

F32 = jnp.float32
BF16 = jnp.bfloat16
I32 = jnp.int32

D_MODEL = 2048
CHUNK = 128
A_GROUPS = 8
A_WIDTH = 1024
HEAD_DIM = 128
B_PATTERNS = ((128, 1), (512, 4), (2048, 16))
HEADS_PER_GROUP = 4
B_HEADS = len(B_PATTERNS) * HEADS_PER_GROUP
B_QKV = B_HEADS * HEAD_DIM
B_OUT = HEADS_PER_GROUP * HEAD_DIM
ATTN_SCALE = HEAD_DIM ** -0.5
N_EXPERTS = 32
TOP_K = 4
D_FF = 2048
SWIGLU_LIMIT = 7.0
SWIGLU_ALPHA = 1.702
EPS = 1e-6
SQRT_HALF = 0.7071067811865476

COL_UV = 0
COL_Q = 2 * A_WIDTH
COL_K = COL_Q + B_QKV
COL_V = COL_K + B_QKV
COL_GATE = COL_V + B_QKV

LANES = 128
MOE_SUPER = 768
MOE_SUB = 256
MOE_TF = 512
MIB = 1 << 20


def _params(semantics, vmem_mib):
    return pltpu.CompilerParams(dimension_semantics=semantics, vmem_limit_bytes=vmem_mib * MIB)


def _rmsnorm_kernel(x_ref, g_ref, o_ref):
    x = x_ref[...]
    y = x * lax.rsqrt(jnp.mean(x * x, axis=-1, keepdims=True) + EPS)
    o_ref[...] = (y * g_ref[...]).astype(o_ref.dtype)


def _rmsnorm_bf16(x, g, tm):
    t, d = x.shape
    return pl.pallas_call(
        _rmsnorm_kernel,
        grid=(t // tm,),
        in_specs=[pl.BlockSpec((tm, d), lambda i: (i, 0)), pl.BlockSpec((1, d), lambda i: (0, 0))],
        out_specs=pl.BlockSpec((tm, d), lambda i: (i, 0)),
        out_shape=jax.ShapeDtypeStruct((t, d), BF16),
        compiler_params=_params(("parallel",), 32),
        name="rmsnorm1",
    )(x, g.reshape(1, d))


def _cast_kernel(w_ref, o_ref):
    o_ref[...] = w_ref[...].astype(o_ref.dtype)


def _to_bf16(w, tr=256):
    r, c = w.shape
    return pl.pallas_call(
        _cast_kernel,
        grid=(r // tr,),
        in_specs=[pl.BlockSpec((tr, c), lambda i: (i, 0))],
        out_specs=pl.BlockSpec((tr, c), lambda i: (i, 0)),
        out_shape=jax.ShapeDtypeStruct((r, c), BF16),
        compiler_params=_params(("parallel",), 32),
        name="cast_bf16",
    )(w)


def _ep_gelu(acc):
    return 0.5 * acc * (1.0 + lax.erf(acc * SQRT_HALF))


def _ep_identity(acc):
    return acc


def _ep_headnorm(acc, gain_ref):
    outs = []
    for h in range(acc.shape[1] // HEAD_DIM):
        a = acc[:, h * HEAD_DIM:(h + 1) * HEAD_DIM]
        ms = jnp.mean(a * a, axis=-1, keepdims=True)
        outs.append(a * lax.rsqrt(ms + EPS) * gain_ref[...])
    return jnp.concatenate(outs, axis=1)


def _ep_gate(acc, bias_ref):
    return jax.nn.sigmoid(acc + bias_ref[...])


def _proj_kernel(h_ref, w_ref, *rest, epilogue, n_extra):
    extra = rest[:n_extra]
    o_ref = rest[n_extra]
    wb_ref = rest[n_extra + 1]

    @pl.when(pl.program_id(1) == 0)
    def _():
        wb_ref[...] = w_ref[...].astype(BF16)

    tm = h_ref.shape[0]
    step = min(tm, 256)
    for r0 in range(0, tm, step):
        acc = jnp.dot(h_ref[r0:r0 + step, :], wb_ref[...], preferred_element_type=F32)
        o_ref[r0:r0 + step, :] = epilogue(acc, *extra).astype(o_ref.dtype)


def _in_proj(h, w, col0, ncols, epilogue, extras, extra_specs, out_dtype, tm, name, tn=512):
    t, k = h.shape
    j0 = col0 // tn
    return pl.pallas_call(
        functools.partial(_proj_kernel, epilogue=epilogue, n_extra=len(extras)),
        grid=(ncols // tn, t // tm),
        in_specs=[pl.BlockSpec((tm, k), lambda j, i: (i, 0)),
                  pl.BlockSpec((k, tn), lambda j, i: (0, j0 + j))] + extra_specs,
        out_specs=pl.BlockSpec((tm, tn), lambda j, i: (i, j)),
        out_shape=jax.ShapeDtypeStruct((t, ncols), out_dtype),
        scratch_shapes=[pltpu.VMEM((k, tn), BF16)],
        compiler_params=_params(("arbitrary", "arbitrary"), 48),
        name=name,
    )(h, w, *extras)


def _token_mixer_inputs(x, norm1, w_in, b_in_gate, q_gain, k_gain, tm, tag):
    h = _rmsnorm_bf16(x, norm1, min(tm, 512))
    gain_spec = [pl.BlockSpec((1, HEAD_DIM), lambda j, i: (0, 0))]
    uv = _in_proj(h, w_in, COL_UV, 2 * A_WIDTH, _ep_gelu, [], [], BF16, tm, "proj_uv_" + tag)
    q = _in_proj(h, w_in, COL_Q, B_QKV, _ep_headnorm, [q_gain.reshape(1, HEAD_DIM)], gain_spec, F32, tm,
                 "proj_q_" + tag)
    k = _in_proj(h, w_in, COL_K, B_QKV, _ep_headnorm, [k_gain.reshape(1, HEAD_DIM)], gain_spec, F32, tm,
                 "proj_k_" + tag)
    v = _in_proj(h, w_in, COL_V, B_QKV, _ep_identity, [], [], F32, tm, "proj_v_" + tag)
    gates = _in_proj(h, w_in, COL_GATE, 2 * D_MODEL, _ep_gate, [b_in_gate.reshape(1, 2 * D_MODEL)],
                     [pl.BlockSpec((1, 512), lambda j, i: (0, j))], BF16, tm, "proj_gate_" + tag)
    return uv, q, k, v, gates


def _gmlp_kernel(uv_ref, lng_ref, lnb_ref, w_ref, bias_ref, a_ref, *vn_out, sample, rows):
    v = uv_ref[:, A_WIDTH:].astype(F32)
    xc = v - jnp.mean(v, axis=-1, keepdims=True)
    vn = xc * lax.rsqrt(jnp.mean(xc * xc, axis=-1, keepdims=True) + EPS) * lng_ref[...] + lnb_ref[...]
    if vn_out:
        vn_out[0][...] = vn
    vnb = vn.astype(BF16)
    row = lax.broadcasted_iota(I32, (CHUNK, CHUNK), 0)
    col = lax.broadcasted_iota(I32, (CHUNK, CHUNK), 1)
    if sample:
        mask = ((row >> 3) == (col >> 3)) & ((row & 7) >= (col & 7))
    else:
        mask = row >= col
    for g in range(A_GROUPS):
        wm = jnp.where(mask, w_ref[g], 0.0).astype(BF16)
        bcol = bias_ref[:, g:g + 1]
        cs = slice(g * CHUNK, (g + 1) * CHUNK)
        for c in range(rows // CHUNK):
            rs = slice(c * CHUNK, (c + 1) * CHUNK)
            s = jnp.dot(wm, vnb[rs, cs], preferred_element_type=F32) + bcol
            a_ref[rs, cs] = (uv_ref[rs, cs].astype(F32) * s).astype(a_ref.dtype)


def _gmlp(uv, ln_g, ln_b, w_mix, bias_t, sample, tag, rows=256):
    t = uv.shape[0]
    out_shape = [jax.ShapeDtypeStruct((t, A_WIDTH), BF16)]
    out_specs = [pl.BlockSpec((rows, A_WIDTH), lambda i: (i, 0))]
    if sample:
        out_shape.append(jax.ShapeDtypeStruct((t, A_WIDTH), F32))
        out_specs.append(pl.BlockSpec((rows, A_WIDTH), lambda i: (i, 0)))
    res = pl.pallas_call(
        functools.partial(_gmlp_kernel, sample=sample, rows=rows),
        grid=(t // rows,),
        in_specs=[pl.BlockSpec((rows, 2 * A_WIDTH), lambda i: (i, 0)),
                  pl.BlockSpec((1, A_WIDTH), lambda i: (0, 0)),
                  pl.BlockSpec((1, A_WIDTH), lambda i: (0, 0)),
                  pl.BlockSpec((A_GROUPS, CHUNK, CHUNK), lambda i: (0, 0, 0)),
                  pl.BlockSpec((CHUNK, A_GROUPS), lambda i: (0, 0))],
        out_specs=out_specs,
        out_shape=out_shape,
        compiler_params=_params(("parallel",), 32),
        name="gmlp_" + tag,
    )(uv, ln_g.reshape(1, A_WIDTH), ln_b.reshape(1, A_WIDTH), w_mix, bias_t)
    return res if sample else res[0]


def _attn_prompt_kernel(*refs, dil, chunks):
    nh = HEADS_PER_GROUP
    q_refs, k_refs, v_refs = refs[:nh], refs[nh:2 * nh], refs[2 * nh:3 * nh]
    o_ref, lse_ref, kprev, vprev, obuf = refs[3 * nh:]
    n = pl.program_id(1)
    L = CHUNK

    @pl.when(n == 0)
    def _():
        kprev[...] = jnp.zeros_like(kprev)
        vprev[...] = jnp.zeros_like(vprev)

    qi = lax.broadcasted_iota(I32, (L, 2 * L), 0)
    ki = lax.broadcasted_iota(I32, (L, 2 * L), 1)
    dist = qi + L - ki
    in_window = (dist >= 0) & (dist <= L)
    first_key = jnp.where(n > 0, 0, L)
    in_window_first = in_window & (ki >= first_key)
    lane = lax.broadcasted_iota(I32, (L, LANES), 1)

    def one_chunk(rows, prev, valid):
        lse_tile = jnp.zeros((L, LANES), F32)
        own = []
        for h in range(nh):
            qr = q_refs[h][0, rows, :].astype(BF16)
            kr = k_refs[h][0, rows, :].astype(BF16)
            vr = v_refs[h][0, rows, :].astype(BF16)
            own.append((kr, vr))
            kcat = jnp.concatenate([prev[h][0], kr], axis=0)
            vcat = jnp.concatenate([prev[h][1], vr], axis=0)
            s = lax.dot_general(qr, kcat, (((1,), (1,)), ((), ())), preferred_element_type=F32) * ATTN_SCALE
            s = jnp.where(valid, s, -jnp.inf)
            m = jnp.max(s, axis=-1, keepdims=True)
            p = jnp.exp(s - m)
            den = jnp.sum(p, axis=-1, keepdims=True)
            obuf[h, rows, :] = jnp.dot(p.astype(BF16), vcat, preferred_element_type=F32) / den
            lse_tile = jnp.where(lane == h, m + jnp.log(den), lse_tile)
        lse_ref[0, rows, :] = lse_tile
        return own

    def load_prev(r):
        return [(kprev[r, :, h * HEAD_DIM:(h + 1) * HEAD_DIM], vprev[r, :, h * HEAD_DIM:(h + 1) * HEAD_DIM])
                for h in range(nh)]

    def store_prev(r, own):
        for h in range(nh):
            kprev[r, :, h * HEAD_DIM:(h + 1) * HEAD_DIM] = own[h][0]
            vprev[r, :, h * HEAD_DIM:(h + 1) * HEAD_DIM] = own[h][1]

    if dil == 1:
        prev = load_prev(0)
        for c in range(chunks):
            prev = one_chunk(pl.ds(c * L, L), prev, in_window_first if c == 0 else in_window)
        store_prev(0, prev)
    else:
        def residue(r, carry):
            store_prev(r, one_chunk(pl.ds(r, L, stride=dil), load_prev(r), in_window_first))
            return carry

        lax.fori_loop(0, dil, residue, 0, unroll=4)
    for h in range(nh):
        o_ref[0, :, h * HEAD_DIM:(h + 1) * HEAD_DIM] = obuf[h]


def _attn_prompt(q, k, v, g, batch, seq):
    _, dil = B_PATTERNS[g]
    chunks = 4 if dil == 1 else 1
    rows = CHUNK * dil * chunks
    nh = HEADS_PER_GROUP
    head_specs = [pl.BlockSpec((1, rows, HEAD_DIM), functools.partial(lambda b, n, c: (b, n, c), c=g * nh + h))
                  for h in range(nh)]
    out_map = lambda b, n: (b, n, 0)
    q3, k3, v3 = (t.reshape(batch, seq, B_QKV) for t in (q, k, v))
    o, lse = pl.pallas_call(
        functools.partial(_attn_prompt_kernel, dil=dil, chunks=chunks),
        grid=(batch, seq // rows),
        in_specs=head_specs * 3,
        out_specs=[pl.BlockSpec((1, rows, B_OUT), out_map), pl.BlockSpec((1, rows, LANES), out_map)],
        out_shape=[jax.ShapeDtypeStruct((batch, seq, B_OUT), F32),
                   jax.ShapeDtypeStruct((batch, seq, LANES), F32)],
        scratch_shapes=[pltpu.VMEM((dil, CHUNK, B_OUT), BF16), pltpu.VMEM((dil, CHUNK, B_OUT), BF16),
                        pltpu.VMEM((nh, rows, HEAD_DIM), F32)],
        compiler_params=_params(("parallel", "arbitrary"), 48),
        name="attn_prompt_g%d" % g,
    )(*([q3] * nh + [k3] * nh + [v3] * nh))
    return o.reshape(batch * seq, B_OUT), lse.reshape(batch * seq, LANES)


def _attn_sample_kernel(q_ref, k_ref, v_ref, c1_ref, c2_ref, c3_ref, b_ref):
    S = q_ref.shape[1]
    nh = HEADS_PER_GROUP
    for s in range(S):
        outs, lses = [], []
        for g, (win, dil) in enumerate(B_PATTERNS):
            hs = slice(g * nh, (g + 1) * nh)
            cref = (c1_ref, c2_ref, c3_ref)[g]
            m0 = s // dil
            new_rows = list(range(s % dil, s + 1, dil))
            kparts = [cref[0, 0, m0:, s % dil, 0]] + [k_ref[0, j:j + 1, hs, :] for j in new_rows]
            vparts = [cref[0, 0, m0:, s % dil, 1]] + [v_ref[0, j:j + 1, hs, :] for j in new_rows]
            kk = jnp.concatenate(kparts, axis=0)
            vv = jnp.concatenate(vparts, axis=0)
            sc = jnp.sum(kk * (q_ref[0, s:s + 1, hs, :] * ATTN_SCALE), axis=-1, keepdims=True)
            m = jnp.max(sc, axis=0, keepdims=True)
            p = jnp.exp(sc - m)
            den = jnp.sum(p, axis=0, keepdims=True)
            outs.append(jnp.sum(p * vv, axis=0, keepdims=True) / den)
            lses.append(m + jnp.log(den))
        mx = jnp.maximum(jnp.maximum(lses[0], lses[1]), lses[2])
        e = [jnp.exp(l - mx) for l in lses]
        den = e[0] + e[1] + e[2]
        b_ref[0, s:s + 1] = (e[0] / den) * outs[0] + (e[1] / den) * outs[1] + (e[2] / den) * outs[2]


def _attn_sample(q, k, v, caches, batch, seq):
    nh = HEADS_PER_GROUP
    assert seq <= CHUNK
    specs, views = [], []
    for c, (win, dil) in zip(caches, B_PATTERNS):
        assert c.shape[0] == 1 and c.shape[2] == win, "decode cache must hold exactly the pattern's window"
        nres = min(dil, seq)
        views.append(c.reshape(batch, 1, win // dil, dil, 2, nh, HEAD_DIM))
        specs.append(pl.BlockSpec((1, 1, win // dil, nres, 2, nh, HEAD_DIM), lambda b: (b, 0, 0, 0, 0, 0, 0)))
    qkv_spec = pl.BlockSpec((1, seq, B_HEADS, HEAD_DIM), lambda b: (b, 0, 0, 0))
    q4, k4, v4 = (t.reshape(batch, seq, B_HEADS, HEAD_DIM) for t in (q, k, v))
    return pl.pallas_call(
        _attn_sample_kernel,
        grid=(batch,),
        in_specs=[qkv_spec] * 3 + specs,
        out_specs=pl.BlockSpec((1, seq, nh, HEAD_DIM), lambda b: (b, 0, 0, 0)),
        out_shape=jax.ShapeDtypeStruct((batch, seq, nh, HEAD_DIM), F32),
        compiler_params=_params(("parallel",), 40),
        name="attn_sample",
    )(q4, k4, v4, *views)


def _combine_groups(o_refs, l_refs):
    comb = []
    for h in range(HEADS_PER_GROUP):
        cs = slice(h * HEAD_DIM, (h + 1) * HEAD_DIM)
        ls = [l[:, h:h + 1] for l in l_refs]
        mx = jnp.maximum(jnp.maximum(ls[0], ls[1]), ls[2])
        e = [jnp.exp(l - mx) for l in ls]
        den = e[0] + e[1] + e[2]
        comb.append(sum((e[g] / den) * o_refs[g][:, cs] for g in range(3)))
    return jnp.concatenate(comb, axis=1).astype(BF16)


def _mix_kernel(*refs, combine):
    if combine:
        a_ref = refs[0]
        b = _combine_groups(refs[1:4], refs[4:7])
        ga_ref, gb_ref, wa_ref, wb_ref, m_ref = refs[7:]
    else:
        a_ref, b_ref, ga_ref, gb_ref, wa_ref, wb_ref, m_ref = refs
        b = b_ref[...].astype(BF16)
    ap = jnp.dot(a_ref[...], wa_ref[...], preferred_element_type=F32)
    bp = jnp.dot(b, wb_ref[...], preferred_element_type=F32)
    m_ref[...] = (ga_ref[...].astype(F32) * ap + gb_ref[...].astype(F32) * bp).astype(m_ref.dtype)


def _mix(a, b_parts, gates, wa, wb, tm, tag):
    t = a.shape[0]
    combine = len(b_parts) > 1
    row = lambda w: pl.BlockSpec((tm, w), lambda i: (i, 0))
    if combine:
        b_specs = [row(B_OUT)] * 3 + [row(LANES)] * 3
    else:
        b_specs = [row(B_OUT)]
    return pl.pallas_call(
        functools.partial(_mix_kernel, combine=combine),
        grid=(t // tm,),
        in_specs=[row(A_WIDTH)] + b_specs + [
            pl.BlockSpec((tm, D_MODEL), lambda i: (i, 0)),
            pl.BlockSpec((tm, D_MODEL), lambda i: (i, 1)),
            pl.BlockSpec((A_WIDTH, D_MODEL), lambda i: (0, 0)),
            pl.BlockSpec((B_OUT, D_MODEL), lambda i: (0, 0))],
        out_specs=row(D_MODEL),
        out_shape=jax.ShapeDtypeStruct((t, D_MODEL), BF16),
        compiler_params=_params(("parallel",), 48),
        name="mix_" + tag,
    )(a, *b_parts, gates, gates, wa, wb)


def _out_kernel(xp_ref, xs_ref, mp_ref, ms_ref, wo_ref, g2_ref, wr_ref, rb_ref, x1_ref, h2_ref, lg_ref, *, n_prompt):
    is_prompt = pl.program_id(0) < n_prompt
    x = jnp.where(is_prompt, xp_ref[...], xs_ref[...])
    m = jnp.where(is_prompt, mp_ref[...], ms_ref[...])
    x1 = x + jnp.dot(m, wo_ref[...], preferred_element_type=F32)
    x1_ref[...] = x1
    h = x1 * lax.rsqrt(jnp.mean(x1 * x1, axis=-1, keepdims=True) + EPS) * g2_ref[...]
    h_hi = h.astype(BF16)
    h_lo = (h - h_hi.astype(F32)).astype(BF16)
    w = wr_ref[...]
    w_hi = w.astype(BF16)
    w_lo = (w - w_hi.astype(F32)).astype(BF16)
    lg = jnp.dot(h_hi, w_hi, preferred_element_type=F32)
    lg += jnp.dot(h_lo, w_hi, preferred_element_type=F32)
    lg += jnp.dot(h_hi, w_lo, preferred_element_type=F32)
    lg_ref[...] = lg + rb_ref[...]
    h2_ref[...] = h


def _out_proj(xp, xs, mp, ms, wo, norm2, router_w, router_b):
    tp, ts = xp.shape[0], xs.shape[0]
    tm = ts
    assert tp % tm == 0
    n_prompt = tp // tm
    t_all = tp + ts
    const = lambda r, c: pl.BlockSpec((r, c), lambda i: (0, 0))
    p_spec = pl.BlockSpec((tm, D_MODEL), lambda i: (jnp.minimum(i, n_prompt - 1), 0))
    s_spec = pl.BlockSpec((tm, D_MODEL), lambda i: (0, 0))
    return pl.pallas_call(
        functools.partial(_out_kernel, n_prompt=n_prompt),
        grid=(n_prompt + 1,),
        in_specs=[p_spec, s_spec, p_spec, s_spec,
                  const(D_MODEL, D_MODEL), const(1, D_MODEL), const(D_MODEL, N_EXPERTS), const(1, N_EXPERTS)],
        out_specs=[pl.BlockSpec((tm, D_MODEL), lambda i: (i, 0)),
                   pl.BlockSpec((tm, D_MODEL), lambda i: (i, 0)),
                   pl.BlockSpec((tm, N_EXPERTS), lambda i: (i, 0))],
        out_shape=[jax.ShapeDtypeStruct((t_all, D_MODEL), F32),
                   jax.ShapeDtypeStruct((t_all, D_MODEL), F32),
                   jax.ShapeDtypeStruct((t_all, N_EXPERTS), F32)],
        compiler_params=_params(("arbitrary",), 48),
        name="out_proj",
    )(xp, xs, mp, ms, wo, norm2.reshape(1, D_MODEL), router_w, router_b.reshape(1, N_EXPERTS))


def _route_kernel(lg_ref, idx_ref, gate_ref, cnt_ref, *, rows):
    @pl.when(pl.program_id(0) == 0)
    def _():
        cnt_ref[...] = jnp.zeros_like(cnt_ref)

    l = lg_ref[...]
    lane = lax.broadcasted_iota(I32, l.shape, 1)
    vals, idxs = [], []
    for _ in range(TOP_K):
        m = jnp.max(l, axis=-1, keepdims=True)
        idx = jnp.min(jnp.where(l == m, lane, N_EXPERTS), axis=-1, keepdims=True)
        vals.append(m)
        idxs.append(idx)
        l = jnp.where(lane == idx, -jnp.inf, l)
    es = [jnp.exp(v - vals[0]) for v in vals]
    den = es[0] + es[1] + es[2] + es[3]
    onehot = sum((lane == idx).astype(F32) for idx in idxs)
    r = lax.broadcasted_iota(I32, (rows, rows), 0)
    c = lax.broadcasted_iota(I32, (rows, rows), 1)
    tri = (r > c).astype(BF16)
    before = jnp.dot(tri, onehot.astype(BF16), preferred_element_type=F32) + cnt_ref[...]
    cnt_ref[...] += jnp.sum(onehot, axis=0, keepdims=True)
    wide = lax.broadcasted_iota(I32, (rows, LANES), 1)
    idx_tile = jnp.zeros((rows, LANES), I32)
    gate_tile = jnp.zeros((rows, LANES), F32)
    for k in range(TOP_K):
        rank = jnp.sum(jnp.where(lane == idxs[k], before, 0.0), axis=-1, keepdims=True).astype(I32)
        idx_tile = jnp.where(wide == k, idxs[k], idx_tile)
        idx_tile = jnp.where(wide == TOP_K + k, rank, idx_tile)
        gate_tile = jnp.where(wide == k, es[k] / den, gate_tile)
    idx_ref[...] = idx_tile
    gate_ref[...] = gate_tile


def _route(logits, rows=256):
    t = logits.shape[0]
    return pl.pallas_call(
        functools.partial(_route_kernel, rows=rows),
        grid=(t // rows,),
        in_specs=[pl.BlockSpec((rows, N_EXPERTS), lambda i: (i, 0))],
        out_specs=[pl.BlockSpec((rows, LANES), lambda i: (i, 0)),
                   pl.BlockSpec((rows, LANES), lambda i: (i, 0)),
                   pl.BlockSpec((1, N_EXPERTS), lambda i: (0, 0))],
        out_shape=[jax.ShapeDtypeStruct((t, LANES), I32),
                   jax.ShapeDtypeStruct((t, LANES), F32),
                   jax.ShapeDtypeStruct((1, N_EXPERTS), F32)],
        compiler_params=_params(("arbitrary",), 32),
        name="route",
    )(logits)


def _slot_kernel(idx_ref, tab_ref, dest_ref):
    idx = idx_ref[...]
    rows = idx.shape[0]
    lane = lax.broadcasted_iota(I32, (rows, LANES), 1)
    e = jnp.where(lane < TOP_K, idx, 0)
    rank = pltpu.roll(idx, LANES - TOP_K, 1)
    base, extra, start = (jnp.take_along_axis(jnp.broadcast_to(tab_ref[r:r + 1, :], (rows, LANES)), e, axis=1,
                                              mode="promise_in_bounds") for r in range(3))
    base = jnp.maximum(base, 1.0)
    sub = (rank >> (MOE_SUB.bit_length() - 1)).astype(F32)
    in_big = sub < extra * (base + 1.0)
    num = jnp.where(in_big, sub, sub - extra * (base + 1.0))
    den = jnp.where(in_big, base + 1.0, base)
    quo = jnp.floor((num + 0.5) / den)
    blk = jnp.where(in_big, quo, extra + quo)
    off = num - quo * den
    dest_ref[...] = ((start + blk) * MOE_SUPER + off * MOE_SUB).astype(I32) + (rank & (MOE_SUB - 1))


def _slots(idx_tile, tables, rows=256):
    t = idx_tile.shape[0]
    return pl.pallas_call(
        _slot_kernel,
        grid=(t // rows,),
        in_specs=[pl.BlockSpec((rows, LANES), lambda i: (i, 0)),
                  pl.BlockSpec(tables.shape, lambda i: (0, 0))],
        out_specs=pl.BlockSpec((rows, LANES), lambda i: (i, 0)),
        out_shape=jax.ShapeDtypeStruct((t, LANES), I32),
        compiler_params=_params(("parallel",), 32),
        name="slots",
    )(idx_tile, tables)


def _dispatch_kernel(dest_ref, h_ref, xs_hbm, sem, *, tm):
    def body(t, carry):
        src = h_ref.at[pl.ds(t, 1)]
        for k in range(TOP_K):
            pltpu.make_async_copy(src, xs_hbm.at[pl.ds(dest_ref[0, 0, t * TOP_K + k], 1)], sem).start(priority=k % 2)
        return carry

    lax.fori_loop(0, tm, body, 0)
    for k in range(TOP_K):
        pltpu.make_async_copy(h_ref, xs_hbm.at[pl.ds(0, tm)], sem).wait()


def _dispatch(h2, dest, n_slots, tm=1280):
    t, w = h2.shape
    assert t % tm == 0
    return pl.pallas_call(
        functools.partial(_dispatch_kernel, tm=tm),
        grid=(t // tm,),
        in_specs=[pl.BlockSpec((1, 1, tm * TOP_K), lambda i: (i, 0, 0), memory_space=pltpu.SMEM),
                  pl.BlockSpec((tm, w), lambda i: (i, 0))],
        out_specs=pl.BlockSpec(memory_space=pl.ANY),
        out_shape=jax.ShapeDtypeStruct((n_slots, w), h2.dtype),
        scratch_shapes=[pltpu.SemaphoreType.DMA],
        compiler_params=_params(("arbitrary",), 32),
        name="dispatch",
    )(dest.reshape(t // tm, 1, tm * TOP_K), h2)


def _moe_kernel(be_ref, bx_ref, nv_ref, x_ref, wg_ref, wu_ref, wd_ref, bg_ref, bu_ref, bd_ref, o_ref, xb_ref):
    del be_ref, bx_ref
    f = pl.program_id(1)
    nv = nv_ref[pl.program_id(0)]

    @pl.when((f == 0) & (nv > 0))
    def _():
        o_ref[...] = jnp.broadcast_to(bd_ref[0], o_ref.shape)

    def ffn(nrows):
        @pl.when(f == 0)
        def _():
            xb_ref[pl.ds(0, nrows), :] = x_ref[pl.ds(0, nrows), :].astype(BF16)

        rows = pl.ds(0, nrows)
        xb = xb_ref[rows, :]
        g = jnp.dot(xb, wg_ref[0].astype(BF16), preferred_element_type=F32) + bg_ref[0]
        u = jnp.dot(xb, wu_ref[0].astype(BF16), preferred_element_type=F32) + bu_ref[0]
        g = jnp.minimum(g, SWIGLU_LIMIT)
        u = jnp.clip(u, -SWIGLU_LIMIT, SWIGLU_LIMIT)
        act = ((u + 1.0) * (g * jax.nn.sigmoid(SWIGLU_ALPHA * g))).astype(BF16)
        o_ref[rows, :] += jnp.dot(act, wd_ref[0].astype(BF16), preferred_element_type=F32)

    for j in range(1, MOE_SUPER // MOE_SUB + 1):
        @pl.when(nv == j)
        def _():
            ffn(j * MOE_SUB)


def _moe(xs, blk_e, blk_x, blk_nv, w_gate, b_gate, w_up, b_up, w_down, b_down):
    n_super = xs.shape[0] // MOE_SUPER
    n_f = D_FF // MOE_TF
    last_f = n_f - 1

    def f_of(m, f, bx, nv):
        flip = (bx[m] & 1) == 1
        return jnp.where(nv[m] > 0, jnp.where(flip, last_f - f, f), jnp.where(flip, 0, last_f))

    grid_spec = pltpu.PrefetchScalarGridSpec(
        num_scalar_prefetch=3,
        grid=(n_super, n_f),
        in_specs=[
            pl.BlockSpec((MOE_SUPER, D_MODEL), lambda m, f, be, bx, nv: (bx[m], 0)),
            pl.BlockSpec((1, D_MODEL, MOE_TF), lambda m, f, be, bx, nv: (be[m], 0, f_of(m, f, bx, nv))),
            pl.BlockSpec((1, D_MODEL, MOE_TF), lambda m, f, be, bx, nv: (be[m], 0, f_of(m, f, bx, nv))),
            pl.BlockSpec((1, MOE_TF, D_MODEL), lambda m, f, be, bx, nv: (be[m], f_of(m, f, bx, nv), 0)),
            pl.BlockSpec((1, 1, MOE_TF), lambda m, f, be, bx, nv: (be[m], 0, f_of(m, f, bx, nv))),
            pl.BlockSpec((1, 1, MOE_TF), lambda m, f, be, bx, nv: (be[m], 0, f_of(m, f, bx, nv))),
            pl.BlockSpec((1, 1, D_MODEL), lambda m, f, be, bx, nv: (be[m], 0, 0)),
        ],
        out_specs=pl.BlockSpec((MOE_SUPER, D_MODEL), lambda m, f, be, bx, nv: (bx[m], 0)),
        scratch_shapes=[pltpu.VMEM((MOE_SUPER, D_MODEL), BF16)],
    )
    return pl.pallas_call(
        _moe_kernel,
        grid_spec=grid_spec,
        out_shape=jax.ShapeDtypeStruct(xs.shape, F32),
        compiler_params=_params(("arbitrary", "arbitrary"), 60),
        name="moe_ffn",
    )(blk_e, blk_x, blk_nv, xs, w_gate, w_up, w_down,
      b_gate.reshape(N_EXPERTS, 1, D_FF), b_up.reshape(N_EXPERTS, 1, D_FF), b_down.reshape(N_EXPERTS, 1, D_MODEL))


def _combine_kernel(dest_ref, dest_next_ref, gate_ref, x1_ref, y_hbm, o_ref, buf, sems, *, tm, n_steps):
    i = pl.program_id(0)
    slot = i % 2

    def gather(dref, s):
        def body(t, carry):
            for k in range(TOP_K):
                pltpu.make_async_copy(y_hbm.at[pl.ds(dref[0, 0, t * TOP_K + k], 1)],
                                      buf.at[s, k, pl.ds(t, 1)], sems.at[s]).start(priority=k % 2)
            return carry

        lax.fori_loop(0, tm, body, 0)

    @pl.when(i == 0)
    def _():
        gather(dest_ref, 0)

    @pl.when(i + 1 < n_steps)
    def _():
        gather(dest_next_ref, 1 - slot)

    for k in range(TOP_K):
        pltpu.make_async_copy(y_hbm.at[pl.ds(0, tm)], buf.at[slot, k], sems.at[slot]).wait()
    acc = x1_ref[...]
    for k in range(TOP_K):
        acc = acc + gate_ref[:, k:k + 1] * buf[slot, k]
    o_ref[...] = acc


def _combine(dest, gates, x1, y, row0, t, tm, tag):
    t_all = x1.shape[0]
    blk0 = row0 // tm
    n_steps = t // tm
    dest3 = dest.reshape(t_all // tm, 1, tm * TOP_K)
    smem = lambda imap: pl.BlockSpec((1, 1, tm * TOP_K), imap, memory_space=pltpu.SMEM)
    return pl.pallas_call(
        functools.partial(_combine_kernel, tm=tm, n_steps=n_steps),
        grid=(n_steps,),
        in_specs=[smem(lambda i: (blk0 + i, 0, 0)),
                  smem(lambda i: (blk0 + jnp.minimum(i + 1, n_steps - 1), 0, 0)),
                  pl.BlockSpec((tm, LANES), lambda i: (blk0 + i, 0)),
                  pl.BlockSpec((tm, D_MODEL), lambda i: (blk0 + i, 0)),
                  pl.BlockSpec(memory_space=pl.ANY)],
        out_specs=pl.BlockSpec((tm, D_MODEL), lambda i: (i, 0)),
        out_shape=jax.ShapeDtypeStruct((t, D_MODEL), F32),
        scratch_shapes=[pltpu.VMEM((2, TOP_K, tm, D_MODEL), F32), pltpu.SemaphoreType.DMA((2,))],
        compiler_params=_params(("arbitrary",), 40),
        name="combine_" + tag,
    )(dest3, dest3, gates, x1, y)


def _kv_state(k, v, g, batch, seq, keep):
    cs = slice(g * B_OUT, (g + 1) * B_OUT)
    k4 = k.reshape(batch, seq, B_QKV)[:, seq - keep:, cs].reshape(batch, keep, HEADS_PER_GROUP, HEAD_DIM)
    v4 = v.reshape(batch, seq, B_QKV)[:, seq - keep:, cs].reshape(batch, keep, HEADS_PER_GROUP, HEAD_DIM)
    return jnp.stack([k4, v4], axis=2)[None]


def kernel(x_prompt, x_sample, cache_kv_w128, cache_kv_w512, cache_kv_w2048, norm1, w_in, b_in_gate, gmlp_ln_g,
           gmlp_ln_b, gmlp_w_s, gmlp_b_s, q_gain, k_gain, w_a_out, w_b_out, w_o, norm2, router_w, router_b,
           exp_w_gate, exp_b_gate, exp_w_up, exp_b_up, exp_w_down, exp_b_down):
    assert norm1.shape[0] == 1, "single trunk layer"
    bp, sp, _ = x_prompt.shape
    bs, ss, _ = x_sample.shape
    tp, ts = bp * sp, bs * ss
    t_all = tp + ts
    xp = x_prompt.reshape(tp, D_MODEL)
    xs_ = x_sample.reshape(ts, D_MODEL)
    w_in2 = w_in[0]

    wa = _to_bf16(w_a_out[0])
    wb = _to_bf16(w_b_out[0])
    wo = _to_bf16(w_o[0])

    uv_p, q_p, k_p, v_p, gates_p = _token_mixer_inputs(xp, norm1[0], w_in2, b_in_gate[0], q_gain[0], k_gain[0],
                                                       2048, "p")
    a_p = _gmlp(uv_p, gmlp_ln_g[0], gmlp_ln_b[0], gmlp_w_s[0], gmlp_b_s[0].T, False, "p")
    o_parts, l_parts = [], []
    for g in range(len(B_PATTERNS)):
        o, lse = _attn_prompt(q_p, k_p, v_p, g, bp, sp)
        o_parts.append(o)
        l_parts.append(lse)
    m_p = _mix(a_p, o_parts + l_parts, gates_p, wa, wb, 512, "p")

    uv_s, q_s, k_s, v_s, gates_s = _token_mixer_inputs(xs_, norm1[0], w_in2, b_in_gate[0], q_gain[0], k_gain[0],
                                                       ts, "s")
    rep = CHUNK // ss
    w_mix_s = jnp.tile(gmlp_w_s[0][:, :ss, :ss], (1, rep, rep))
    bias_s = jnp.tile(gmlp_b_s[0][:, :ss].T, (rep, 1))
    a_s, vn_s = _gmlp(uv_s, gmlp_ln_g[0], gmlp_ln_b[0], w_mix_s, bias_s, True, "s")
    b_s = _attn_sample(q_s, k_s, v_s, (cache_kv_w128, cache_kv_w512, cache_kv_w2048), bs, ss)
    m_s = _mix(a_s, [b_s.reshape(ts, B_OUT)], gates_s, wa, wb, ts, "s")

    x1, h2, logits = _out_proj(xp, xs_, m_p, m_s, wo, norm2[0], router_w[0], router_b[0])
    y_p, y_s = _moe_layer(x1, h2, logits, tp, ts, exp_w_gate[0], exp_b_gate[0], exp_w_up[0], exp_b_up[0],
                          exp_w_down[0], exp_b_down[0])

    keep = [min(w, sp) for w, _ in B_PATTERNS]
    return (y_p.reshape(bp, sp, D_MODEL),
            y_s.reshape(bs, ss, D_MODEL),
            _kv_state(k_p, v_p, 0, bp, sp, keep[0]),
            _kv_state(k_p, v_p, 1, bp, sp, keep[1]),
            _kv_state(k_p, v_p, 2, bp, sp, keep[2]),
            _kv_state(k_s, v_s, 0, bs, ss, ss),
            _kv_state(k_s, v_s, 1, bs, ss, ss),
            _kv_state(k_s, v_s, 2, bs, ss, ss),
            vn_s.reshape(1, bs, ss, A_WIDTH))


def _moe_layer(x1, h2, logits, tp, ts, w_gate, b_gate, w_up, b_up, w_down, b_down):
    t_all = tp + ts
    idx_tile, gate_tile, counts = _route(logits)
    assert MOE_SUB & (MOE_SUB - 1) == 0
    counts = counts[0].astype(I32)
    sub_per_blk = MOE_SUPER // MOE_SUB
    nsub = (counts + MOE_SUB - 1) // MOE_SUB
    nblk = (nsub + sub_per_blk - 1) // sub_per_blk
    base = nsub // jnp.maximum(nblk, 1)
    extra = nsub - base * nblk
    blk_end = jnp.cumsum(nblk)
    blk_start = blk_end - nblk
    n_super = (t_all * TOP_K + N_EXPERTS * (MOE_SUPER - 1)) // MOE_SUPER
    tables = jnp.zeros((8, LANES), F32).at[:3, :N_EXPERTS].set(jnp.stack([base, extra, blk_start]).astype(F32))
    dest = _slots(idx_tile, tables)[:, :TOP_K]
    sb = jnp.arange(n_super, dtype=I32)
    sb_valid = sb < blk_end[-1]
    sb_e = jnp.minimum(jnp.sum((blk_end[None, :] <= sb[:, None]).astype(I32), axis=1), N_EXPERTS - 1)
    sb_nv = jnp.where(sb_valid, base[sb_e] + ((sb - blk_start[sb_e]) < extra[sb_e]).astype(I32), 0).astype(I32)
    last = blk_end[-1] - 1
    sb_x = jnp.where(sb_valid, sb, last).astype(I32)
    sb_e = jnp.where(sb_valid, sb_e, sb_e[last]).astype(I32)

    xs_sorted = _dispatch(h2, dest, n_super * MOE_SUPER)
    y_sorted = _moe(xs_sorted, sb_e, sb_x, sb_nv, w_gate, b_gate, w_up, b_up, w_down, b_down)
    y_p = _combine(dest, gate_tile, x1, y_sorted, 0, tp, 256, "p")
    y_s = _combine(dest, gate_tile, x1, y_sorted, tp, ts, ts, "s")
    return y_p, y_s
```
